```python
import jax, jax.numpy as jnp
from jax import lax
import numpy as np

D_MODEL = 1024
BATCH = 2
SEQ = 16384
DEPTH = 1
DEC_BATCH = 16
DEC_SEQ = 16
PAST_LEN = 4096

CHUNK = 64
D_MIX = 2 * D_MODEL
C_CONV = D_MIX // 2
CONV_W = 31
SSM_HEADS = 16
SSM_HEAD_DIM = (D_MIX // 2) // SSM_HEADS
D_SSM = SSM_HEADS * SSM_HEAD_DIM
SSM_GROUPS = 4
D_STATE = 128
SSM_CONV_W = 4
D_XBC = D_SSM + 2 * SSM_GROUPS * D_STATE
D_FF = ((8 * D_MODEL // 3 + 127) // 128) * 128
FFN_CONV_W = 3
D_IN_PROJ = 2 * C_CONV + D_SSM + D_XBC + SSM_HEADS
EPS = 1e-6

kernel_name = "hymba_style_conformer_conv_ssd_convffn_step"


def rmsnorm(x, g):
    x32 = x.astype(jnp.float32)
    y = x32 * lax.rsqrt(jnp.mean(x32 * x32, axis=-1, keepdims=True) + EPS)
    return (y * g.astype(jnp.float32)).astype(x.dtype)


def layernorm(x, g, b):
    x32 = x.astype(jnp.float32)
    mu = jnp.mean(x32, axis=-1, keepdims=True)
    xc = x32 - mu
    var = jnp.mean(xc * xc, axis=-1, keepdims=True)
    return (xc * lax.rsqrt(var + EPS) * g.astype(jnp.float32) + b.astype(jnp.float32)).astype(x.dtype)


def causal_dwconv(x, prefix, w, b):
    xp = jnp.concatenate([prefix.astype(x.dtype), x], axis=1)
    out = lax.conv_general_dilated(
        xp, w[:, None, :].astype(x.dtype), window_strides=(1,), padding="VALID",
        dimension_numbers=("NWC", "WIO", "NWC"), feature_group_count=x.shape[-1])
    return out + b.astype(x.dtype), xp[:, -(w.shape[0] - 1):]


def ssd_scan(x, dt, A, Bm, Cm, h0, block):
    b, l, h, p = x.shape
    g, n = Bm.shape[2], Bm.shape[3]
    r = h // g
    c = l // block
    x = x.reshape(b, c, block, g, r, p)
    dt = dt.reshape(b, c, block, g, r)
    a = dt * A.reshape(g, r)
    xdt = x * dt[..., None]
    Bm = Bm.reshape(b, c, block, g, n)
    Cm = Cm.reshape(b, c, block, g, n)
    a_cum = jnp.cumsum(a, axis=2)
    seg = a_cum[:, :, :, None] - a_cum[:, :, None, :]
    causal = jnp.tril(jnp.ones((block, block), dtype=bool))[None, None, :, :, None, None]
    decay = jnp.exp(jnp.where(causal, seg, -jnp.inf))
    cb = jnp.einsum('bctgn,bcsgn->bctsg', Cm, Bm)
    y_diag = jnp.einsum('bctsgr,bcsgrp->bctgrp', cb[..., None] * decay, xdt)
    decay_to_end = jnp.exp(a_cum[:, :, -1:] - a_cum)
    states = jnp.einsum('bcsgn,bcsgrp->bcgrpn', Bm, xdt * decay_to_end[..., None])
    block_decay = jnp.exp(a_cum[:, :, -1])

    def step(hc, inp):
        st, dec = inp
        return hc * dec[..., None, None] + st, hc

    h_final, h_prev = lax.scan(step, h0.reshape(b, g, r, p, n),
                               (jnp.moveaxis(states, 1, 0), jnp.moveaxis(block_decay, 1, 0)))
    h_prev = jnp.moveaxis(h_prev, 0, 1)
    y_off = jnp.einsum('bctgn,bcgrpn->bctgrp', Cm, h_prev) * jnp.exp(a_cum)[..., None]
    y = (y_diag + y_off).reshape(b, l, h, p)
    return y, h_final.reshape(b, h, p, n)


def hybrid_layer(x, buf_a, buf_xbc, h0, buf_ffn,
                 norm_mix, w_in, conv_a_w, conv_a_b, ln_a_g, ln_a_b,
                 ssm_conv_w, ssm_conv_b, dt_bias, a_log, d_skip, ssm_norm, w_out,
                 norm_ffn, w_up, ffn_conv_w, ffn_conv_b, w_down):
    bsz, l, _ = x.shape
    hn = rmsnorm(x, norm_mix)
    proj = hn @ w_in
    cuts = (C_CONV, 2 * C_CONV, 2 * C_CONV + D_SSM, 2 * C_CONV + D_SSM + D_XBC)
    a_val, a_gate, z, xbc, dt_raw = jnp.split(proj, cuts, axis=-1)
    u = a_val * jax.nn.sigmoid(a_gate)
    u_conv, new_buf_a = causal_dwconv(u, buf_a, conv_a_w, conv_a_b)
    out_a = jax.nn.silu(layernorm(u_conv, ln_a_g, ln_a_b))
    xbc_c, new_buf_xbc = causal_dwconv(xbc, buf_xbc, ssm_conv_w, ssm_conv_b)
    xbc_c = jax.nn.silu(xbc_c).astype(jnp.float32)
    xs, Bm, Cm = jnp.split(xbc_c, (D_SSM, D_SSM + SSM_GROUPS * D_STATE), axis=-1)
    xs = xs.reshape(bsz, l, SSM_HEADS, SSM_HEAD_DIM)
    Bm = Bm.reshape(bsz, l, SSM_GROUPS, D_STATE)
    Cm = Cm.reshape(bsz, l, SSM_GROUPS, D_STATE)
    dt = jax.nn.softplus(dt_raw.astype(jnp.float32) + dt_bias.astype(jnp.float32))
    A = -jnp.exp(a_log.astype(jnp.float32))
    block = CHUNK if l % CHUNK == 0 else l
    y, new_h = ssd_scan(xs, dt, A, Bm, Cm, h0.astype(jnp.float32), block)
    y = y + d_skip.astype(jnp.float32)[:, None] * xs
    y = y.reshape(bsz, l, D_SSM).astype(x.dtype)
    out_b = rmsnorm(y * jax.nn.silu(z), ssm_norm)
    x = x + jnp.concatenate([out_a, out_b], axis=-1) @ w_out
    hf = rmsnorm(x, norm_ffn)
    gate, val = jnp.split(hf @ w_up, 2, axis=-1)
    gate_c, new_buf_ffn = causal_dwconv(gate, buf_ffn, ffn_conv_w, ffn_conv_b)
    x = x + (jax.nn.silu(gate_c) * val) @ w_down
    return x, new_buf_a, new_buf_xbc, new_h, new_buf_ffn


def setup_inputs(seed: int = 0) -> dict:
    key = jax.random.key(seed)
    k = jax.random.split(key, 32)
    f32 = jnp.float32

    def nrm(kk, shape, scale):
        return jax.random.normal(kk, shape, f32) * scale

    def gain(kk, shape):
        return 1.0 + 0.02 * jax.random.normal(kk, shape, f32)

    dt0 = jnp.exp(jax.random.uniform(k[0], (DEPTH, SSM_HEADS), f32, np.log(1e-3), np.log(1e-1)))
    dt_bias = dt0 + jnp.log(-jnp.expm1(-dt0))
    a_log = jnp.log(jax.random.uniform(k[1], (DEPTH, SSM_HEADS), f32, 1.0, 16.0))
    return {
        "x_prompt": nrm(k[2], (BATCH, SEQ, D_MODEL), 1.0),
        "x_sample": nrm(k[3], (DEC_BATCH, DEC_SEQ, D_MODEL), 1.0),
        "cache_conv_a": nrm(k[4], (DEPTH, DEC_BATCH, CONV_W - 1, C_CONV), 0.5),
        "cache_ssm_conv": nrm(k[5], (DEPTH, DEC_BATCH, SSM_CONV_W - 1, D_XBC), 1.0),
        "state_ssm": nrm(k[6], (DEPTH, DEC_BATCH, SSM_HEADS, SSM_HEAD_DIM, D_STATE), 0.1),
        "cache_ffn_conv": nrm(k[7], (DEPTH, DEC_BATCH, FFN_CONV_W - 1, D_FF), 1.0),
        "norm_mix": gain(k[8], (DEPTH, D_MODEL)),
        "w_in": nrm(k[9], (DEPTH, D_MODEL, D_IN_PROJ), D_MODEL ** -0.5),
        "conv_a_w": nrm(k[10], (DEPTH, CONV_W, C_CONV), CONV_W ** -0.5),
        "conv_a_b": nrm(k[11], (DEPTH, C_CONV), 0.02),
        "ln_a_g": gain(k[12], (DEPTH, C_CONV)),
        "ln_a_b": nrm(k[13], (DEPTH, C_CONV), 0.02),
        "ssm_conv_w": nrm(k[14], (DEPTH, SSM_CONV_W, D_XBC), SSM_CONV_W ** -0.5),
        "ssm_conv_b": nrm(k[15], (DEPTH, D_XBC), 0.02),
        "dt_bias": dt_bias,
        "a_log": a_log,
        "d_skip": gain(k[16], (DEPTH, SSM_HEADS)),
        "ssm_norm": gain(k[17], (DEPTH, D_SSM)),
        "w_out": nrm(k[18], (DEPTH, D_MIX, D_MODEL), D_MIX ** -0.5),
        "norm_ffn": gain(k[19], (DEPTH, D_MODEL)),
        "w_up": nrm(k[20], (DEPTH, D_MODEL, 2 * D_FF), D_MODEL ** -0.5),
        "ffn_conv_w": nrm(k[21], (DEPTH, FFN_CONV_W, D_FF), FFN_CONV_W ** -0.5),
        "ffn_conv_b": nrm(k[22], (DEPTH, D_FF), 0.02),
        "w_down": nrm(k[23], (DEPTH, D_FF, D_MODEL), D_FF ** -0.5),
        "norm_final": gain(k[24], (D_MODEL,)),
    }


def reference(x_prompt, x_sample, cache_conv_a, cache_ssm_conv, state_ssm, cache_ffn_conv,
              norm_mix, w_in, conv_a_w, conv_a_b, ln_a_g, ln_a_b,
              ssm_conv_w, ssm_conv_b, dt_bias, a_log, d_skip, ssm_norm, w_out,
              norm_ffn, w_up, ffn_conv_w, ffn_conv_b, w_down, norm_final):
    bp = x_prompt.shape[0]
    xp, xs = x_prompt, x_sample
    pa, pb, ph, pf = [], [], [], []
    sa, sb, sh, sf = [], [], [], []
    for i in range(DEPTH):
        wts = (norm_mix[i], w_in[i], conv_a_w[i], conv_a_b[i], ln_a_g[i], ln_a_b[i],
               ssm_conv_w[i], ssm_conv_b[i], dt_bias[i], a_log[i], d_skip[i], ssm_norm[i], w_out[i],
               norm_ffn[i], w_up[i], ffn_conv_w[i], ffn_conv_b[i], w_down[i])
        xp, na, nb, nh, nf = hybrid_layer(
            xp,
            jnp.zeros((bp, CONV_W - 1, C_CONV), xp.dtype),
            jnp.zeros((bp, SSM_CONV_W - 1, D_XBC), xp.dtype),
            jnp.zeros((bp, SSM_HEADS, SSM_HEAD_DIM, D_STATE), jnp.float32),
            jnp.zeros((bp, FFN_CONV_W - 1, D_FF), xp.dtype),
            *wts)
        pa.append(na); pb.append(nb); ph.append(nh); pf.append(nf)
        xs, ma, mb, mh, mf = hybrid_layer(
            xs, cache_conv_a[i], cache_ssm_conv[i], state_ssm[i], cache_ffn_conv[i], *wts)
        sa.append(ma); sb.append(mb); sh.append(mh); sf.append(mf)
    y_prompt = rmsnorm(xp, norm_final)
    y_sample = rmsnorm(xs, norm_final)
    return (y_prompt, y_sample,
            jnp.stack(pa), jnp.stack(pb), jnp.stack(ph), jnp.stack(pf),
            jnp.stack(sa), jnp.stack(sb), jnp.stack(sh), jnp.stack(sf))
```

```python
import functools

import jax
import jax.numpy as jnp
from jax import lax
from jax.experimental import pallas as pl
from jax.experimental.pallas import tpu as pltpu

D_MODEL = 1024
C_CONV = 1024
CONV_W = 31
SSM_HEADS = 16
SSM_HEAD_DIM = 64
D_SSM = SSM_HEADS * SSM_HEAD_DIM
SSM_GROUPS = 4
HEADS_PER_GROUP = SSM_HEADS // SSM_GROUPS
D_STATE = 128
SSM_CONV_W = 4
D_XBC = D_SSM + 2 * SSM_GROUPS * D_STATE
D_FF = 2816
FFN_CONV_W = 3
EPS = 1e-6

SUBLANES = 8
LANES = 128
TILE = 256
SSD_BLOCK = LANES
PAD_A = 32
PAD_B = 8
PAD_F = 8
GROUP_W = HEADS_PER_GROUP * SSM_HEAD_DIM
VMEM_LIMIT_BYTES = 56 * 1024 * 1024

F32 = jnp.float32
BF16 = jnp.bfloat16


def _sigmoid(x):
    return jax.nn.sigmoid(x)


def _silu(x):
    return x * _sigmoid(x)


def _rmsnorm(x, g):
    return x * lax.rsqrt(jnp.mean(x * x, axis=-1, keepdims=True) + EPS) * g


def _softplus(x):
    return jnp.maximum(x, 0.0) + jnp.log1p(jnp.exp(-jnp.abs(x)))


def _split_bf16(v):
    hi = v.astype(BF16)
    lo = (v - hi.astype(F32)).astype(BF16)
    return hi, lo


def _dot(a, b):
    return jnp.dot(a, b, preferred_element_type=F32)


def _dot_split_lhs(v, m):
    hi, lo = _split_bf16(v)
    return _dot(hi, m) + _dot(lo, m)


def _dot_split_rhs(m, v):
    hi, lo = _split_bf16(v)
    return _dot(m, hi) + _dot(m, lo)


def _dwconv(ext_ref, j, first_row, rows, width, w_ref, b_ref, ntaps, emit, rb, cb):
    for r0 in range(0, rows, rb):
        for c0 in range(0, width, cb):
            acc = None
            for k in range(ntaps):
                xk = ext_ref[j, pl.ds(first_row + k + r0, rb), pl.ds(c0, cb)]
                term = xk * w_ref[pl.ds(k, 1), pl.ds(c0, cb)]
                acc = term if acc is None else acc + term
            emit(r0, c0, acc + b_ref[:, pl.ds(c0, cb)])


def _segment_masks(row0, seg_rows):
    ti = lax.broadcasted_iota(jnp.int32, (SSD_BLOCK, SSD_BLOCK), 0)
    si = lax.broadcasted_iota(jnp.int32, (SSD_BLOCK, SSD_BLOCK), 1)
    if seg_rows >= SSD_BLOCK:
        same = None
        causal = ti >= si
    else:
        shift = seg_rows.bit_length() - 1
        same = ((ti + row0) >> shift) == ((si + row0) >> shift)
        causal = same & (ti >= si)
    return same, causal


def _ssd_block(c, seg_rows, xbcc_ref, dt_ref, alog_ref, dx_ref, e_ref, state_ref, y_ref):
    q = SSD_BLOCK
    assert seg_rows % q == 0 or q % seg_rows == 0
    row0 = c * q
    rows = pl.ds(row0, q)
    segs = range(row0 // seg_rows, (row0 + q - 1) // seg_rows + 1)
    multi = len(segs) > 1

    lane = lax.broadcasted_iota(jnp.int32, (1, LANES), 1)
    head_lane = lane < SSM_HEADS
    dt_lane = (lane >= SSM_HEADS) & (lane < 2 * SSM_HEADS)

    dt = dt_ref[rows, :]
    a_neg = jnp.where(head_lane, -jnp.exp(alog_ref[...]), 0.0)
    a = dt * a_neg

    same, causal = _segment_masks(row0, seg_rows)
    tri = jnp.where(causal, 1.0, 0.0).astype(BF16)
    acum = _dot_split_rhs(tri, a)
    if multi:
        seg_ones = jnp.where(same, 1.0, 0.0).astype(BF16)
        atot = _dot_split_rhs(seg_ones, a)
    else:
        atot = jnp.broadcast_to(acum[q - 1:q, :], (q, LANES))

    stacked_t = jnp.transpose(acum + jnp.where(dt_lane, dt, 0.0))

    e = e_ref[...]
    ea_x = _dot_split_lhs(jnp.exp(acum), e)
    w_x = _dot_split_lhs(jnp.exp(atot - acum) * dt, e)
    seg_lo = [max(j * seg_rows, row0) - row0 for j in segs]
    first_rows = jnp.concatenate([atot[lo:lo + SUBLANES, :] for lo in seg_lo], axis=0)
    bd_x = _dot_split_lhs(jnp.exp(first_rows), e)

    head_cols = lax.broadcasted_iota(jnp.int32, (1, GROUP_W), 1) // SSM_HEAD_DIM
    if multi:
        row_seg = (lax.broadcasted_iota(jnp.int32, (q, 1), 0) + row0) // seg_rows

    for g in range(SSM_GROUPS):
        gcols = pl.ds(g * GROUP_W, GROUP_W)
        xs_g = xbcc_ref[rows, gcols]
        b_g = xbcc_ref[rows, pl.ds(D_SSM + g * D_STATE, D_STATE)]
        c_g = xbcc_ref[rows, pl.ds(D_SSM + SSM_GROUPS * D_STATE + g * D_STATE, D_STATE)]
        b_bf = b_g.astype(BF16)
        c_bf = c_g.astype(BF16)
        xs_bf = xs_g.astype(BF16)

        cb = lax.dot_general(c_bf, b_bf, (((1,), (1,)), ((), ())), preferred_element_type=F32)
        m_parts = []
        x_parts = []
        for hh in range(HEADS_PER_GROUP):
            h = g * HEADS_PER_GROUP + hh
            seg = acum[:, h:h + 1] - stacked_t[h:h + 1, :]
            decay = jnp.exp(jnp.where(causal, seg, -jnp.inf))
            m = cb * decay * stacked_t[SSM_HEADS + h:SSM_HEADS + h + 1, :]
            m_parts.append(m.astype(BF16))
            x_parts.append(jnp.where(head_cols == hh, xs_bf, jnp.zeros_like(xs_bf)))
        y_g = _dot(jnp.concatenate(m_parts, axis=1), jnp.concatenate(x_parts, axis=0))

        b_t = jnp.transpose(b_g).astype(BF16)
        xdec = xs_g * w_x[:, g * GROUP_W:(g + 1) * GROUP_W]

        off_parts = []
        for i, j in enumerate(segs):
            lo = max(j * seg_rows, row0) - row0
            hi = min((j + 1) * seg_rows, row0 + q) - row0
            h_prev = state_ref[j, :, gcols]
            off_parts.append(_dot(c_bf[lo:hi, :], h_prev.astype(BF16)))
            if multi:
                xm = jnp.where(row_seg == j, xdec, 0.0).astype(BF16)
            else:
                xm = xdec.astype(BF16)
            bd = bd_x[SUBLANES * i:SUBLANES * i + 1, g * GROUP_W:(g + 1) * GROUP_W]
            state_ref[j, :, gcols] = h_prev * bd + _dot(b_t, xm)
        y_off = off_parts[0] if len(off_parts) == 1 else jnp.concatenate(off_parts, axis=0)
        y_g = y_g + y_off * ea_x[:, g * GROUP_W:(g + 1) * GROUP_W] + dx_ref[:, gcols] * xs_g
        y_ref[rows, gcols] = y_g


def _mixer_kernel(*refs, n_seg, seg_rows, has_cache, n_steps):
    it = iter(refs)
    x_ref = next(it)
    if has_cache:
        ca_ref, cb_ref, st_ref = next(it), next(it), next(it)
    (nm_ref, wag_ref, wz_ref, wxbc_ref, wdt_ref, dtb_ref, alog_ref, caw_ref, cab_ref, lng_ref, lnb_ref,
     scw_ref, scb_ref, dx_ref, sn_ref, wout_ref, e_ref) = [next(it) for _ in range(17)]
    x1_ref, ta_ref, tb_ref, state_ref = [next(it) for _ in range(4)]
    h_ref, uext_ref, xext_ref, cva_ref, xbcc_ref, z_ref, dt_ref, y_ref, mix_ref = [next(it) for _ in range(9)]

    s = seg_rows
    t = pl.program_id(1)

    @pl.when(t == 0)
    def _():
        if has_cache:
            uext_ref[:, PAD_A - (CONV_W - 1):PAD_A, :] = ca_ref[...]
            xext_ref[:, PAD_B - (SSM_CONV_W - 1):PAD_B, :] = cb_ref[...]
            state_ref[...] = st_ref[...]
        else:
            uext_ref[:, 0:PAD_A, :] = jnp.zeros((n_seg, PAD_A, C_CONV), F32)
            xext_ref[:, 0:PAD_B, :] = jnp.zeros((n_seg, PAD_B, D_XBC), F32)
            state_ref[...] = jnp.zeros(state_ref.shape, F32)

    x = x_ref[...]
    h_ref[...] = _rmsnorm(x, nm_ref[...]).astype(BF16)
    hb = h_ref[...]

    ag = _dot(hb, wag_ref[...])
    u = ag[:, :C_CONV] * _sigmoid(ag[:, C_CONV:])
    for j in range(n_seg):
        uext_ref[j, PAD_A:PAD_A + s, :] = u[j * s:(j + 1) * s, :]
        ta_ref[j] = uext_ref[j, PAD_A + s - (CONV_W - 1):PAD_A + s, :]

        def emit_a(r0, c0, v, j=j):
            cva_ref[pl.ds(j * s + r0, v.shape[0]), pl.ds(c0, v.shape[1])] = v

        _dwconv(uext_ref, j, PAD_A - (CONV_W - 1), s, C_CONV, caw_ref, cab_ref, CONV_W, emit_a,
                rb=min(s, 64), cb=256)
    uc = cva_ref[...]
    mu = jnp.mean(uc, axis=-1, keepdims=True)
    xc = uc - mu
    var = jnp.mean(xc * xc, axis=-1, keepdims=True)
    ln = xc * lax.rsqrt(var + EPS) * lng_ref[...] + lnb_ref[...]
    mix_ref[:, 0:C_CONV] = _silu(ln).astype(BF16)

    z_ref[...] = _dot(hb, wz_ref[...])
    xbc = _dot(hb, wxbc_ref[...])
    for j in range(n_seg):
        xext_ref[j, PAD_B:PAD_B + s, :] = xbc[j * s:(j + 1) * s, :]
        tb_ref[j] = xext_ref[j, PAD_B + s - (SSM_CONV_W - 1):PAD_B + s, :]

        def emit_b(r0, c0, v, j=j):
            xbcc_ref[pl.ds(j * s + r0, v.shape[0]), pl.ds(c0, v.shape[1])] = _silu(v)

        _dwconv(xext_ref, j, PAD_B - (SSM_CONV_W - 1), s, D_XBC, scw_ref, scb_ref, SSM_CONV_W, emit_b,
                rb=min(s, 64), cb=256)
    dt_ref[...] = _softplus(_dot(hb, wdt_ref[...]) + dtb_ref[...])
    for c in range(n_seg * s // SSD_BLOCK):
        _ssd_block(c, s, xbcc_ref, dt_ref, alog_ref, dx_ref, e_ref, state_ref, y_ref)
    gated = y_ref[...] * _silu(z_ref[...])
    mix_ref[:, C_CONV:] = _rmsnorm(gated, sn_ref[...]).astype(BF16)

    x1_ref[...] = x + _dot(mix_ref[...], wout_ref[...])

    if n_steps > 1:
        uext_ref[0, 0:PAD_A, :] = uext_ref[0, s:s + PAD_A, :]
        xext_ref[0, 0:PAD_B, :] = xext_ref[0, s:s + PAD_B, :]


def _ffn_kernel(*refs, n_seg, seg_rows, has_cache, n_steps, final_norm):
    it = iter(refs)
    x1_ref = next(it)
    if has_cache:
        cf_ref = next(it)
    nf_ref, wg_ref, wv_ref, fcw_ref, fcb_ref, wd_ref = [next(it) for _ in range(6)]
    if final_norm:
        nfin_ref = next(it)
    y_ref, tf_ref = next(it), next(it)
    hf_ref, gext_ref, val_ref, act_ref = [next(it) for _ in range(4)]

    s = seg_rows
    t = pl.program_id(1)

    @pl.when(t == 0)
    def _():
        if has_cache:
            gext_ref[:, PAD_F - (FFN_CONV_W - 1):PAD_F, :] = cf_ref[...]
        else:
            gext_ref[:, 0:PAD_F, :] = jnp.zeros((n_seg, PAD_F, D_FF), F32)

    x1 = x1_ref[...]
    hf_ref[...] = _rmsnorm(x1, nf_ref[...]).astype(BF16)
    hf = hf_ref[...]
    gate = _dot(hf, wg_ref[...])
    val_ref[...] = _dot(hf, wv_ref[...])
    for j in range(n_seg):
        gext_ref[j, PAD_F:PAD_F + s, :] = gate[j * s:(j + 1) * s, :]
        tf_ref[j] = gext_ref[j, PAD_F + s - (FFN_CONV_W - 1):PAD_F + s, :]

        def emit_f(r0, c0, v, j=j):
            rows = pl.ds(j * s + r0, v.shape[0])
            cols = pl.ds(c0, v.shape[1])
            act_ref[rows, cols] = (_silu(v) * val_ref[rows, cols]).astype(BF16)

        _dwconv(gext_ref, j, PAD_F - (FFN_CONV_W - 1), s, D_FF, fcw_ref, fcb_ref, FFN_CONV_W, emit_f,
                rb=min(s, 64), cb=256)
    x2 = x1 + _dot(act_ref[...], wd_ref[...])
    if final_norm:
        x2 = _rmsnorm(x2, nfin_ref[...])
    y_ref[...] = x2

    if n_steps > 1:
        gext_ref[0, 0:PAD_F, :] = gext_ref[0, s:s + PAD_F, :]


def _const_spec(arr):
    nd = arr.ndim
    return pl.BlockSpec(arr.shape, lambda b, t, _nd=nd: (0,) * _nd, pipeline_mode=pl.Buffered(1))


def _layout(n_streams, length):
    if length >= TILE:
        assert length % TILE == 0
        return TILE, 1, TILE, (n_streams, length // TILE)
    tile = SSD_BLOCK
    assert tile % length == 0 and (n_streams * length) % tile == 0 and length % SUBLANES == 0
    n_seg = tile // length
    return tile, n_seg, length, (n_streams // n_seg, 1)


def _mixer(x, caches, p):
    n_streams, length, _ = x.shape
    tile, n_seg, s, grid = _layout(n_streams, length)
    has_cache = caches is not None
    xt = x.reshape(grid[0], grid[1] * tile, D_MODEL)

    tile_spec = pl.BlockSpec((None, tile, D_MODEL), lambda b, t: (b, t, 0))

    def per_batch(shape, **kw):
        return pl.BlockSpec((n_seg,) + shape, lambda b, t: (b, 0, 0), **kw)

    inputs = [xt]
    in_specs = [tile_spec]
    if has_cache:
        inputs += list(caches)
        once = dict(pipeline_mode=pl.Buffered(1))
        in_specs += [per_batch((CONV_W - 1, C_CONV), **once), per_batch((SSM_CONV_W - 1, D_XBC), **once),
                     per_batch((D_STATE, D_SSM), **once)]
    consts = [p["norm_mix"], p["w_ag"], p["w_z"], p["w_xbc"], p["w_dt"], p["dt_bias"], p["a_log"],
              p["conv_a_w"], p["conv_a_b"], p["ln_a_g"], p["ln_a_b"], p["ssm_conv_w"], p["ssm_conv_b"],
              p["d_x"], p["ssm_norm"], p["w_out"], p["expand"]]
    inputs += consts
    in_specs += [_const_spec(c) for c in consts]

    out_shape = (
        jax.ShapeDtypeStruct(xt.shape, F32),
        jax.ShapeDtypeStruct((n_streams, CONV_W - 1, C_CONV), F32),
        jax.ShapeDtypeStruct((n_streams, SSM_CONV_W - 1, D_XBC), F32),
        jax.ShapeDtypeStruct((n_streams, D_STATE, D_SSM), F32),
    )
    out_specs = (tile_spec, per_batch((CONV_W - 1, C_CONV)), per_batch((SSM_CONV_W - 1, D_XBC)),
                 per_batch((D_STATE, D_SSM)))
    scratch = [
        pltpu.VMEM((tile, D_MODEL), BF16),
        pltpu.VMEM((n_seg, PAD_A + s, C_CONV), F32),
        pltpu.VMEM((n_seg, PAD_B + s, D_XBC), F32),
        pltpu.VMEM((tile, C_CONV), F32),
        pltpu.VMEM((tile, D_XBC), F32),
        pltpu.VMEM((tile, D_SSM), F32),
        pltpu.VMEM((tile, LANES), F32),
        pltpu.VMEM((tile, D_SSM), F32),
        pltpu.VMEM((tile, C_CONV + D_SSM), BF16),
    ]
    kern = functools.partial(_mixer_kernel, n_seg=n_seg, seg_rows=s, has_cache=has_cache, n_steps=grid[1])
    x1, ta, tb, st = pl.pallas_call(
        kern, grid=grid, in_specs=in_specs, out_specs=out_specs, out_shape=out_shape,
        scratch_shapes=scratch, name="mixer_cache" if has_cache else "mixer_fresh",
        compiler_params=pltpu.CompilerParams(
            dimension_semantics=("arbitrary", "arbitrary"), vmem_limit_bytes=VMEM_LIMIT_BYTES),
    )(*inputs)
    return x1.reshape(x.shape), ta, tb, st


def _ffn(x1, cache, p, norm_final):
    n_streams, length, _ = x1.shape
    tile, n_seg, s, grid = _layout(n_streams, length)
    has_cache = cache is not None
    xt = x1.reshape(grid[0], grid[1] * tile, D_MODEL)
    tile_spec = pl.BlockSpec((None, tile, D_MODEL), lambda b, t: (b, t, 0))
    tail_spec = pl.BlockSpec((n_seg, FFN_CONV_W - 1, D_FF), lambda b, t: (b, 0, 0))

    inputs = [xt]
    in_specs = [tile_spec]
    if has_cache:
        inputs.append(cache)
        in_specs.append(tail_spec)
    consts = [p["norm_ffn"], p["w_gate"], p["w_val"], p["ffn_conv_w"], p["ffn_conv_b"], p["w_down"]]
    if norm_final is not None:
        consts.append(norm_final)
    inputs += consts
    in_specs += [_const_spec(c) for c in consts]

    out_shape = (jax.ShapeDtypeStruct(xt.shape, F32),
                 jax.ShapeDtypeStruct((n_streams, FFN_CONV_W - 1, D_FF), F32))
    scratch = [
        pltpu.VMEM((tile, D_MODEL), BF16),
        pltpu.VMEM((n_seg, PAD_F + s, D_FF), F32),
        pltpu.VMEM((tile, D_FF), F32),
        pltpu.VMEM((tile, D_FF), BF16),
    ]
    kern = functools.partial(_ffn_kernel, n_seg=n_seg, seg_rows=s, has_cache=has_cache, n_steps=grid[1],
                             final_norm=norm_final is not None)
    y, tf = pl.pallas_call(
        kern, grid=grid, in_specs=in_specs, out_specs=(tile_spec, tail_spec), out_shape=out_shape,
        scratch_shapes=scratch, name="ffn_cache" if has_cache else "ffn_fresh",
        compiler_params=pltpu.CompilerParams(
            dimension_semantics=("arbitrary", "arbitrary"), vmem_limit_bytes=VMEM_LIMIT_BYTES),
    )(*inputs)
    return y.reshape(x1.shape), tf


def _pad_lanes(v, reps=1):
    v = jnp.tile(v.astype(F32), reps)
    return jnp.pad(v, (0, LANES - v.shape[0]))[None, :]


def _layer_params(norm_mix, w_in, conv_a_w, conv_a_b, ln_a_g, ln_a_b, ssm_conv_w, ssm_conv_b, dt_bias, a_log,
                  d_skip, ssm_norm, w_out, norm_ffn, w_up, ffn_conv_w, ffn_conv_b, w_down):
    o_z = 2 * C_CONV
    o_xbc = o_z + D_SSM
    o_dt = o_xbc + D_XBC
    w_dt = w_in[:, o_dt:o_dt + SSM_HEADS]
    w_dt = jnp.pad(jnp.concatenate([w_dt, w_dt], axis=1), ((0, 0), (0, LANES - 2 * SSM_HEADS)))
    head_of_col = jnp.arange(D_SSM, dtype=jnp.int32) // SSM_HEAD_DIM
    expand = (jnp.arange(LANES, dtype=jnp.int32)[:, None] == head_of_col[None, :]).astype(BF16)
    return {
        "norm_mix": norm_mix[None, :],
        "w_ag": w_in[:, :o_z].astype(BF16),
        "w_z": w_in[:, o_z:o_xbc].astype(BF16),
        "w_xbc": w_in[:, o_xbc:o_dt].astype(BF16),
        "w_dt": w_dt.astype(BF16),
        "dt_bias": _pad_lanes(dt_bias, 2),
        "a_log": _pad_lanes(a_log),
        "conv_a_w": conv_a_w, "conv_a_b": conv_a_b[None, :],
        "ln_a_g": ln_a_g[None, :], "ln_a_b": ln_a_b[None, :],
        "ssm_conv_w": ssm_conv_w, "ssm_conv_b": ssm_conv_b[None, :],
        "d_x": jnp.repeat(d_skip, SSM_HEAD_DIM)[None, :],
        "ssm_norm": ssm_norm[None, :],
        "w_out": w_out.astype(BF16),
        "expand": expand,
        "norm_ffn": norm_ffn[None, :],
        "w_gate": w_up[:, :D_FF].astype(BF16),
        "w_val": w_up[:, D_FF:].astype(BF16),
        "ffn_conv_w": ffn_conv_w, "ffn_conv_b": ffn_conv_b[None, :],
        "w_down": w_down.astype(BF16),
    }


def _state_to_kernel(st):
    n = st.shape[0]
    return jnp.transpose(st, (0, 3, 1, 2)).reshape(n, D_STATE, D_SSM)


def _state_from_kernel(st):
    n = st.shape[0]
    return jnp.transpose(st.reshape(n, D_STATE, SSM_HEADS, SSM_HEAD_DIM), (0, 2, 3, 1))


def kernel(x_prompt, x_sample, cache_conv_a, cache_ssm_conv, state_ssm, cache_ffn_conv, norm_mix, w_in, conv_a_w,
           conv_a_b, ln_a_g, ln_a_b, ssm_conv_w, ssm_conv_b, dt_bias, a_log, d_skip, ssm_norm, w_out, norm_ffn,
           w_up, ffn_conv_w, ffn_conv_b, w_down, norm_final):
    depth = w_in.shape[0]
    xp, xs = x_prompt, x_sample
    outs = [[] for _ in range(8)]
    for i in range(depth):
        p = _layer_params(norm_mix[i], w_in[i], conv_a_w[i], conv_a_b[i], ln_a_g[i], ln_a_b[i], ssm_conv_w[i],
                          ssm_conv_b[i], dt_bias[i], a_log[i], d_skip[i], ssm_norm[i], w_out[i], norm_ffn[i],
                          w_up[i], ffn_conv_w[i], ffn_conv_b[i], w_down[i])
        nfin = norm_final[None, :] if i == depth - 1 else None
        xp, pa, pb, ph = _mixer(xp, None, p)
        xp, pf = _ffn(xp, None, p, nfin)
        xs, sa, sb, sh = _mixer(xs, (cache_conv_a[i], cache_ssm_conv[i], _state_to_kernel(state_ssm[i])), p)
        xs, sf = _ffn(xs, cache_ffn_conv[i], p, nfin)
        for lst, v in zip(outs, (pa, pb, _state_from_kernel(ph), pf, sa, sb, _state_from_kernel(sh), sf)):
            lst.append(v)
    return (xp, xs) + tuple(jnp.stack(o) for o in outs)
```

```python
import functools

import jax
import jax.numpy as jnp
from jax import lax
from jax.experimental import pallas as pl
from jax.experimental.pallas import tpu as pltpu

D_MODEL = 1024
C_CONV = 1024
CONV_W = 31
SSM_HEADS = 16
SSM_HEAD_DIM = 64
D_SSM = SSM_HEADS * SSM_HEAD_DIM
SSM_GROUPS = 4
HEADS_PER_GROUP = SSM_HEADS // SSM_GROUPS
D_STATE = 128
SSM_CONV_W = 4
D_XBC = D_SSM + 2 * SSM_GROUPS * D_STATE
D_FF = 2816
FFN_CONV_W = 3
EPS = 1e-6

SUBLANES = 8
LANES = 128
TILE = 256
SSD_BLOCK = LANES
CONV_ROWS = 128
PAD_A = 32
PAD_B = 8
PAD_F = 8
GROUP_W = HEADS_PER_GROUP * SSM_HEAD_DIM
VMEM_LIMIT_BYTES = 56 * 1024 * 1024

F32 = jnp.float32
BF16 = jnp.bfloat16


def _sigmoid(x):
    return jax.nn.sigmoid(x)


def _silu(x):
    return x * _sigmoid(x)


def _rmsnorm(x, g):
    return x * lax.rsqrt(jnp.mean(x * x, axis=-1, keepdims=True) + EPS) * g


def _softplus(x):
    return jnp.maximum(x, 0.0) + jnp.log1p(jnp.exp(-jnp.abs(x)))


def _split_bf16(v):
    hi = v.astype(BF16)
    lo = (v - hi.astype(F32)).astype(BF16)
    return hi, lo


def _dot(a, b):
    return jnp.dot(a, b, preferred_element_type=F32)


def _dot_split_lhs(v, m):
    hi, lo = _split_bf16(v)
    return _dot(hi, m) + _dot(lo, m)


def _dot_split_rhs(m, v):
    hi, lo = _split_bf16(v)
    return _dot(m, hi) + _dot(m, lo)


def _dwconv(ext_ref, j, first_row, rows, width, w_ref, b_ref, ntaps, emit, rb, cb):
    phases = {}
    for k in range(ntaps):
        d = first_row + k
        phases.setdefault(d % SUBLANES, []).append((k, d - d % SUBLANES))
    for r0 in range(0, rows, rb):
        for c0 in range(0, width, cb):
            cols = pl.ds(c0, cb)
            acc = b_ref[:, cols]
            for phase, taps in sorted(phases.items()):
                n = rb if phase == 0 else rb + SUBLANES
                part = None
                for k, base in taps:
                    term = ext_ref[j, pl.ds(r0 + base, n), cols] * w_ref[pl.ds(k, 1), cols]
                    part = term if part is None else part + term
                acc = acc + (part if phase == 0 else part[phase:phase + rb, :])
            emit(r0, c0, acc)


def _segment_masks(row0, seg_rows):
    ti = lax.broadcasted_iota(jnp.int32, (SSD_BLOCK, SSD_BLOCK), 0)
    si = lax.broadcasted_iota(jnp.int32, (SSD_BLOCK, SSD_BLOCK), 1)
    if seg_rows >= SSD_BLOCK:
        same = None
        causal = ti >= si
    else:
        shift = seg_rows.bit_length() - 1
        same = ((ti + row0) >> shift) == ((si + row0) >> shift)
        causal = same & (ti >= si)
    return same, causal


def _ssd_block(c, seg_rows, xbcc_ref, dt_ref, alog_ref, dx_ref, e_ref, state_ref, y_ref):
    q = SSD_BLOCK
    assert seg_rows % q == 0 or q % seg_rows == 0
    row0 = c * q
    rows = pl.ds(row0, q)
    segs = range(row0 // seg_rows, (row0 + q - 1) // seg_rows + 1)
    multi = len(segs) > 1

    lane = lax.broadcasted_iota(jnp.int32, (1, LANES), 1)
    head_lane = lane < SSM_HEADS
    dt_lane = (lane >= SSM_HEADS) & (lane < 2 * SSM_HEADS)

    dt = dt_ref[rows, :]
    a_neg = jnp.where(head_lane, -jnp.exp(alog_ref[...]), 0.0)
    a = dt * a_neg

    same, causal = _segment_masks(row0, seg_rows)
    tri = jnp.where(causal, 1.0, 0.0).astype(BF16)
    acum = _dot_split_rhs(tri, a)
    if multi:
        seg_ones = jnp.where(same, 1.0, 0.0).astype(BF16)
        atot = _dot_split_rhs(seg_ones, a)
    else:
        atot = jnp.broadcast_to(acum[q - 1:q, :], (q, LANES))

    stacked_t = jnp.transpose(acum + jnp.where(dt_lane, dt, 0.0))

    e = e_ref[...]
    ea_x = _dot_split_lhs(jnp.exp(acum), e)
    w_x = _dot_split_lhs(jnp.exp(atot - acum) * dt, e)
    seg_lo = [max(j * seg_rows, row0) - row0 for j in segs]
    first_rows = jnp.concatenate([atot[lo:lo + SUBLANES, :] for lo in seg_lo], axis=0)
    bd_x = _dot_split_lhs(jnp.exp(first_rows), e)

    head_cols = lax.broadcasted_iota(jnp.int32, (1, GROUP_W), 1) // SSM_HEAD_DIM
    if multi:
        row_seg = (lax.broadcasted_iota(jnp.int32, (q, 1), 0) + row0) // seg_rows

    for g in range(SSM_GROUPS):
        gcols = pl.ds(g * GROUP_W, GROUP_W)
        xs_g = xbcc_ref[rows, gcols]
        b_g = xbcc_ref[rows, pl.ds(D_SSM + g * D_STATE, D_STATE)]
        c_g = xbcc_ref[rows, pl.ds(D_SSM + SSM_GROUPS * D_STATE + g * D_STATE, D_STATE)]
        b_bf = b_g.astype(BF16)
        c_bf = c_g.astype(BF16)
        xs_bf = xs_g.astype(BF16)

        cb = lax.dot_general(c_bf, b_bf, (((1,), (1,)), ((), ())), preferred_element_type=F32)
        m_parts = []
        x_parts = []
        for hh in range(HEADS_PER_GROUP):
            h = g * HEADS_PER_GROUP + hh
            seg = acum[:, h:h + 1] - stacked_t[h:h + 1, :]
            decay = jnp.exp(jnp.where(causal, seg, -jnp.inf))
            m = cb * decay * stacked_t[SSM_HEADS + h:SSM_HEADS + h + 1, :]
            m_parts.append(m.astype(BF16))
            x_parts.append(jnp.where(head_cols == hh, xs_bf, jnp.zeros_like(xs_bf)))
        y_g = _dot(jnp.concatenate(m_parts, axis=1), jnp.concatenate(x_parts, axis=0))

        b_t = jnp.transpose(b_g).astype(BF16)
        xdec = xs_g * w_x[:, g * GROUP_W:(g + 1) * GROUP_W]

        off_parts = []
        for i, j in enumerate(segs):
            lo = max(j * seg_rows, row0) - row0
            hi = min((j + 1) * seg_rows, row0 + q) - row0
            h_prev = state_ref[j, :, gcols]
            off_parts.append(_dot(c_bf[lo:hi, :], h_prev.astype(BF16)))
            if multi:
                xm = jnp.where(row_seg == j, xdec, 0.0).astype(BF16)
            else:
                xm = xdec.astype(BF16)
            bd = bd_x[SUBLANES * i:SUBLANES * i + 1, g * GROUP_W:(g + 1) * GROUP_W]
            state_ref[j, :, gcols] = h_prev * bd + _dot(b_t, xm)
        y_off = off_parts[0] if len(off_parts) == 1 else jnp.concatenate(off_parts, axis=0)
        y_g = y_g + y_off * ea_x[:, g * GROUP_W:(g + 1) * GROUP_W] + dx_ref[:, gcols] * xs_g
        y_ref[rows, gcols] = y_g


def _mixer_kernel(*refs, n_seg, seg_rows, has_cache, n_steps):
    it = iter(refs)
    x_ref = next(it)
    if has_cache:
        ca_ref, cb_ref, st_ref = next(it), next(it), next(it)
    (nm_ref, wag_ref, wz_ref, wxbc_ref, wdt_ref, dtb_ref, alog_ref, caw_ref, cab_ref, lng_ref, lnb_ref,
     scw_ref, scb_ref, dx_ref, sn_ref, wout_ref, e_ref) = [next(it) for _ in range(17)]
    x1_ref, ta_ref, tb_ref, state_ref = [next(it) for _ in range(4)]
    h_ref, uext_ref, xext_ref, cva_ref, xbcc_ref, z_ref, dt_ref, y_ref, mix_ref = [next(it) for _ in range(9)]

    s = seg_rows
    t = pl.program_id(1)

    @pl.when(t == 0)
    def _():
        uext_ref[:, 0:PAD_A, :] = jnp.zeros((n_seg, PAD_A, C_CONV), F32)
        xext_ref[:, 0:PAD_B, :] = jnp.zeros((n_seg, PAD_B, D_XBC), F32)
        if has_cache:
            uext_ref[:, PAD_A - (CONV_W - 1):PAD_A, :] = ca_ref[...]
            xext_ref[:, PAD_B - (SSM_CONV_W - 1):PAD_B, :] = cb_ref[...]
            state_ref[...] = st_ref[...]
        else:
            state_ref[...] = jnp.zeros(state_ref.shape, F32)

    x = x_ref[...]
    h_ref[...] = _rmsnorm(x, nm_ref[...]).astype(BF16)
    hb = h_ref[...]

    ag = _dot(hb, wag_ref[...])
    u = ag[:, :C_CONV] * _sigmoid(ag[:, C_CONV:])
    for j in range(n_seg):
        uext_ref[j, PAD_A:PAD_A + s, :] = u[j * s:(j + 1) * s, :]
        ta_ref[j] = uext_ref[j, PAD_A + s - (CONV_W - 1):PAD_A + s, :]

        def emit_a(r0, c0, v, j=j):
            cva_ref[pl.ds(j * s + r0, v.shape[0]), pl.ds(c0, v.shape[1])] = v

        _dwconv(uext_ref, j, PAD_A - (CONV_W - 1), s, C_CONV, caw_ref, cab_ref, CONV_W, emit_a,
                rb=min(s, CONV_ROWS), cb=LANES)
    uc = cva_ref[...]
    mu = jnp.mean(uc, axis=-1, keepdims=True)
    xc = uc - mu
    var = jnp.mean(xc * xc, axis=-1, keepdims=True)
    ln = xc * lax.rsqrt(var + EPS) * lng_ref[...] + lnb_ref[...]
    mix_ref[:, 0:C_CONV] = _silu(ln).astype(BF16)

    z_ref[...] = _dot(hb, wz_ref[...])
    xbc = _dot(hb, wxbc_ref[...])
    for j in range(n_seg):
        xext_ref[j, PAD_B:PAD_B + s, :] = xbc[j * s:(j + 1) * s, :]
        tb_ref[j] = xext_ref[j, PAD_B + s - (SSM_CONV_W - 1):PAD_B + s, :]

        def emit_b(r0, c0, v, j=j):
            xbcc_ref[pl.ds(j * s + r0, v.shape[0]), pl.ds(c0, v.shape[1])] = _silu(v)

        _dwconv(xext_ref, j, PAD_B - (SSM_CONV_W - 1), s, D_XBC, scw_ref, scb_ref, SSM_CONV_W, emit_b,
                rb=min(s, CONV_ROWS), cb=LANES)
    dt_ref[...] = _softplus(_dot(hb, wdt_ref[...]) + dtb_ref[...])
    for c in range(n_seg * s // SSD_BLOCK):
        _ssd_block(c, s, xbcc_ref, dt_ref, alog_ref, dx_ref, e_ref, state_ref, y_ref)
    gated = y_ref[...] * _silu(z_ref[...])
    mix_ref[:, C_CONV:] = _rmsnorm(gated, sn_ref[...]).astype(BF16)

    x1_ref[...] = x + _dot(mix_ref[...], wout_ref[...])

    if n_steps > 1:
        uext_ref[0, 0:PAD_A, :] = uext_ref[0, s:s + PAD_A, :]
        xext_ref[0, 0:PAD_B, :] = xext_ref[0, s:s + PAD_B, :]


def _ffn_kernel(*refs, n_seg, seg_rows, has_cache, n_steps, final_norm):
    it = iter(refs)
    x1_ref = next(it)
    if has_cache:
        cf_ref = next(it)
    nf_ref, wg_ref, wv_ref, fcw_ref, fcb_ref, wd_ref = [next(it) for _ in range(6)]
    if final_norm:
        nfin_ref = next(it)
    y_ref, tf_ref = next(it), next(it)
    hf_ref, gext_ref, val_ref, act_ref = [next(it) for _ in range(4)]

    s = seg_rows
    t = pl.program_id(1)

    @pl.when(t == 0)
    def _():
        gext_ref[:, 0:PAD_F, :] = jnp.zeros((n_seg, PAD_F, D_FF), F32)
        if has_cache:
            gext_ref[:, PAD_F - (FFN_CONV_W - 1):PAD_F, :] = cf_ref[...]

    x1 = x1_ref[...]
    hf_ref[...] = _rmsnorm(x1, nf_ref[...]).astype(BF16)
    hf = hf_ref[...]
    gate = _dot(hf, wg_ref[...])
    val_ref[...] = _dot(hf, wv_ref[...])
    for j in range(n_seg):
        gext_ref[j, PAD_F:PAD_F + s, :] = gate[j * s:(j + 1) * s, :]
        tf_ref[j] = gext_ref[j, PAD_F + s - (FFN_CONV_W - 1):PAD_F + s, :]

        def emit_f(r0, c0, v, j=j):
            rows = pl.ds(j * s + r0, v.shape[0])
            cols = pl.ds(c0, v.shape[1])
            act_ref[rows, cols] = (_silu(v) * val_ref[rows, cols]).astype(BF16)

        _dwconv(gext_ref, j, PAD_F - (FFN_CONV_W - 1), s, D_FF, fcw_ref, fcb_ref, FFN_CONV_W, emit_f,
                rb=min(s, CONV_ROWS), cb=LANES)
    x2 = x1 + _dot(act_ref[...], wd_ref[...])
    if final_norm:
        x2 = _rmsnorm(x2, nfin_ref[...])
    y_ref[...] = x2

    if n_steps > 1:
        gext_ref[0, 0:PAD_F, :] = gext_ref[0, s:s + PAD_F, :]


def _const_spec(arr):
    nd = arr.ndim
    return pl.BlockSpec(arr.shape, lambda b, t, _nd=nd: (0,) * _nd, pipeline_mode=pl.Buffered(1))


def _layout(n_streams, length):
    if length >= TILE:
        assert length % TILE == 0
        return TILE, 1, TILE, (n_streams, length // TILE)
    tile = SSD_BLOCK
    assert tile % length == 0 and (n_streams * length) % tile == 0 and length % SUBLANES == 0
    n_seg = tile // length
    return tile, n_seg, length, (n_streams // n_seg, 1)


def _mixer(x, caches, p):
    n_streams, length, _ = x.shape
    tile, n_seg, s, grid = _layout(n_streams, length)
    has_cache = caches is not None
    xt = x.reshape(grid[0], grid[1] * tile, D_MODEL)

    tile_spec = pl.BlockSpec((None, tile, D_MODEL), lambda b, t: (b, t, 0))

    def per_batch(shape, **kw):
        return pl.BlockSpec((n_seg,) + shape, lambda b, t: (b, 0, 0), **kw)

    inputs = [xt]
    in_specs = [tile_spec]
    if has_cache:
        inputs += list(caches)
        once = dict(pipeline_mode=pl.Buffered(1))
        in_specs += [per_batch((CONV_W - 1, C_CONV), **once), per_batch((SSM_CONV_W - 1, D_XBC), **once),
                     per_batch((D_STATE, D_SSM), **once)]
    consts = [p["norm_mix"], p["w_ag"], p["w_z"], p["w_xbc"], p["w_dt"], p["dt_bias"], p["a_log"],
              p["conv_a_w"], p["conv_a_b"], p["ln_a_g"], p["ln_a_b"], p["ssm_conv_w"], p["ssm_conv_b"],
              p["d_x"], p["ssm_norm"], p["w_out"], p["expand"]]
    inputs += consts
    in_specs += [_const_spec(c) for c in consts]

    out_shape = (
        jax.ShapeDtypeStruct(xt.shape, F32),
        jax.ShapeDtypeStruct((n_streams, CONV_W - 1, C_CONV), F32),
        jax.ShapeDtypeStruct((n_streams, SSM_CONV_W - 1, D_XBC), F32),
        jax.ShapeDtypeStruct((n_streams, D_STATE, D_SSM), F32),
    )
    out_specs = (tile_spec, per_batch((CONV_W - 1, C_CONV)), per_batch((SSM_CONV_W - 1, D_XBC)),
                 per_batch((D_STATE, D_SSM)))
    scratch = [
        pltpu.VMEM((tile, D_MODEL), BF16),
        pltpu.VMEM((n_seg, PAD_A + s, C_CONV), F32),
        pltpu.VMEM((n_seg, PAD_B + s, D_XBC), F32),
        pltpu.VMEM((tile, C_CONV), F32),
        pltpu.VMEM((tile, D_XBC), F32),
        pltpu.VMEM((tile, D_SSM), F32),
        pltpu.VMEM((tile, LANES), F32),
        pltpu.VMEM((tile, D_SSM), F32),
        pltpu.VMEM((tile, C_CONV + D_SSM), BF16),
    ]
    kern = functools.partial(_mixer_kernel, n_seg=n_seg, seg_rows=s, has_cache=has_cache, n_steps=grid[1])
    x1, ta, tb, st = pl.pallas_call(
        kern, grid=grid, in_specs=in_specs, out_specs=out_specs, out_shape=out_shape,
        scratch_shapes=scratch, name="mixer_cache" if has_cache else "mixer_fresh",
        compiler_params=pltpu.CompilerParams(
            dimension_semantics=("arbitrary", "arbitrary"), vmem_limit_bytes=VMEM_LIMIT_BYTES),
    )(*inputs)
    return x1.reshape(x.shape), ta, tb, st


def _ffn(x1, cache, p, norm_final):
    n_streams, length, _ = x1.shape
    tile, n_seg, s, grid = _layout(n_streams, length)
    has_cache = cache is not None
    xt = x1.reshape(grid[0], grid[1] * tile, D_MODEL)
    tile_spec = pl.BlockSpec((None, tile, D_MODEL), lambda b, t: (b, t, 0))
    tail_spec = pl.BlockSpec((n_seg, FFN_CONV_W - 1, D_FF), lambda b, t: (b, 0, 0))

    inputs = [xt]
    in_specs = [tile_spec]
    if has_cache:
        inputs.append(cache)
        in_specs.append(tail_spec)
    consts = [p["norm_ffn"], p["w_gate"], p["w_val"], p["ffn_conv_w"], p["ffn_conv_b"], p["w_down"]]
    if norm_final is not None:
        consts.append(norm_final)
    inputs += consts
    in_specs += [_const_spec(c) for c in consts]

    out_shape = (jax.ShapeDtypeStruct(xt.shape, F32),
                 jax.ShapeDtypeStruct((n_streams, FFN_CONV_W - 1, D_FF), F32))
    scratch = [
        pltpu.VMEM((tile, D_MODEL), BF16),
        pltpu.VMEM((n_seg, PAD_F + s, D_FF), F32),
        pltpu.VMEM((tile, D_FF), F32),
        pltpu.VMEM((tile, D_FF), BF16),
    ]
    kern = functools.partial(_ffn_kernel, n_seg=n_seg, seg_rows=s, has_cache=has_cache, n_steps=grid[1],
                             final_norm=norm_final is not None)
    y, tf = pl.pallas_call(
        kern, grid=grid, in_specs=in_specs, out_specs=(tile_spec, tail_spec), out_shape=out_shape,
        scratch_shapes=scratch, name="ffn_cache" if has_cache else "ffn_fresh",
        compiler_params=pltpu.CompilerParams(
            dimension_semantics=("arbitrary", "arbitrary"), vmem_limit_bytes=VMEM_LIMIT_BYTES),
    )(*inputs)
    return y.reshape(x1.shape), tf


def _pad_lanes(v, reps=1):
    v = jnp.tile(v.astype(F32), reps)
    return jnp.pad(v, (0, LANES - v.shape[0]))[None, :]


def _layer_params(norm_mix, w_in, conv_a_w, conv_a_b, ln_a_g, ln_a_b, ssm_conv_w, ssm_conv_b, dt_bias, a_log,
                  d_skip, ssm_norm, w_out, norm_ffn, w_up, ffn_conv_w, ffn_conv_b, w_down):
    o_z = 2 * C_CONV
    o_xbc = o_z + D_SSM
    o_dt = o_xbc + D_XBC
    w_dt = w_in[:, o_dt:o_dt + SSM_HEADS]
    w_dt = jnp.pad(jnp.concatenate([w_dt, w_dt], axis=1), ((0, 0), (0, LANES - 2 * SSM_HEADS)))
    head_of_col = jnp.arange(D_SSM, dtype=jnp.int32) // SSM_HEAD_DIM
    expand = (jnp.arange(LANES, dtype=jnp.int32)[:, None] == head_of_col[None, :]).astype(BF16)
    return {
        "norm_mix": norm_mix[None, :],
        "w_ag": w_in[:, :o_z].astype(BF16),
        "w_z": w_in[:, o_z:o_xbc].astype(BF16),
        "w_xbc": w_in[:, o_xbc:o_dt].astype(BF16),
        "w_dt": w_dt.astype(BF16),
        "dt_bias": _pad_lanes(dt_bias, 2),
        "a_log": _pad_lanes(a_log),
        "conv_a_w": conv_a_w, "conv_a_b": conv_a_b[None, :],
        "ln_a_g": ln_a_g[None, :], "ln_a_b": ln_a_b[None, :],
        "ssm_conv_w": ssm_conv_w, "ssm_conv_b": ssm_conv_b[None, :],
        "d_x": jnp.repeat(d_skip, SSM_HEAD_DIM)[None, :],
        "ssm_norm": ssm_norm[None, :],
        "w_out": w_out.astype(BF16),
        "expand": expand,
        "norm_ffn": norm_ffn[None, :],
        "w_gate": w_up[:, :D_FF].astype(BF16),
        "w_val": w_up[:, D_FF:].astype(BF16),
        "ffn_conv_w": ffn_conv_w, "ffn_conv_b": ffn_conv_b[None, :],
        "w_down": w_down.astype(BF16),
    }


def _state_to_kernel(st):
    n = st.shape[0]
    return jnp.transpose(st, (0, 3, 1, 2)).reshape(n, D_STATE, D_SSM)


def _state_from_kernel(st):
    n = st.shape[0]
    return jnp.transpose(st.reshape(n, D_STATE, SSM_HEADS, SSM_HEAD_DIM), (0, 2, 3, 1))


def kernel(x_prompt, x_sample, cache_conv_a, cache_ssm_conv, state_ssm, cache_ffn_conv, norm_mix, w_in, conv_a_w,
           conv_a_b, ln_a_g, ln_a_b, ssm_conv_w, ssm_conv_b, dt_bias, a_log, d_skip, ssm_norm, w_out, norm_ffn,
           w_up, ffn_conv_w, ffn_conv_b, w_down, norm_final):
    depth = w_in.shape[0]
    xp, xs = x_prompt, x_sample
    outs = [[] for _ in range(8)]
    for i in range(depth):
        p = _layer_params(norm_mix[i], w_in[i], conv_a_w[i], conv_a_b[i], ln_a_g[i], ln_a_b[i], ssm_conv_w[i],
                          ssm_conv_b[i], dt_bias[i], a_log[i], d_skip[i], ssm_norm[i], w_out[i], norm_ffn[i],
                          w_up[i], ffn_conv_w[i], ffn_conv_b[i], w_down[i])
        nfin = norm_final[None, :] if i == depth - 1 else None
        xp, pa, pb, ph = _mixer(xp, None, p)
        xp, pf = _ffn(xp, None, p, nfin)
        xs, sa, sb, sh = _mixer(xs, (cache_conv_a[i], cache_ssm_conv[i], _state_to_kernel(state_ssm[i])), p)
        xs, sf = _ffn(xs, cache_ffn_conv[i], p, nfin)
        for lst, v in zip(outs, (pa, pb, _state_from_kernel(ph), pf, sa, sb, _state_from_kernel(sh), sf)):
            lst.append(v)
    return (xp, xs) + tuple(jnp.stack(o) for o in outs)
```

```python
import functools

import jax
import jax.numpy as jnp
from jax import lax
from jax.experimental import pallas as pl
from jax.experimental.pallas import tpu as pltpu

D_MODEL = 1024
C_CONV = 1024
CONV_W = 31
SSM_HEADS = 16
SSM_HEAD_DIM = 64
D_SSM = SSM_HEADS * SSM_HEAD_DIM
SSM_GROUPS = 4
HEADS_PER_GROUP = SSM_HEADS // SSM_GROUPS
D_STATE = 128
SSM_CONV_W = 4
D_XBC = D_SSM + 2 * SSM_GROUPS * D_STATE
D_FF = 2816
FFN_CONV_W = 3
EPS = 1e-6
LOG2_E = 1.4426950408889634

SUBLANES = 8
LANES = 128
TILE = 256
SSD_BLOCK = LANES
CONV_ROWS = 128
PAD_A = 32
PAD_B = 8
PAD_F = 8
GROUP_W = HEADS_PER_GROUP * SSM_HEAD_DIM
VMEM_LIMIT_BYTES = 56 * 1024 * 1024

F32 = jnp.float32
BF16 = jnp.bfloat16


def _sigmoid(x):
    return jax.nn.sigmoid(x)


def _silu(x):
    return x * _sigmoid(x)


def _rmsnorm(x, g):
    return x * lax.rsqrt(jnp.mean(x * x, axis=-1, keepdims=True) + EPS) * g


def _softplus(x):
    return jnp.maximum(x, 0.0) + jnp.log1p(jnp.exp(-jnp.abs(x)))


def _split_bf16(v):
    hi = v.astype(BF16)
    lo = (v - hi.astype(F32)).astype(BF16)
    return hi, lo


def _dot(a, b):
    return jnp.dot(a, b, preferred_element_type=F32)


def _dot_split_lhs(v, m):
    hi, lo = _split_bf16(v)
    return _dot(hi, m) + _dot(lo, m)


def _dot_split_rhs(m, v):
    hi, lo = _split_bf16(v)
    return _dot(m, hi) + _dot(m, lo)


def _unpack(wp_ref):
    return pltpu.bitcast(wp_ref[...], BF16)


def _hist_init(ext_ref, pad, cache_ref):
    n_seg, n_slab = ext_ref.shape[0], ext_ref.shape[1]
    ext_ref[:, :, 0:pad, :] = jnp.zeros((n_seg, n_slab, pad, LANES), F32)
    if cache_ref is not None:
        ntail = cache_ref.shape[1]
        for c in range(n_slab):
            ext_ref[:, c, pad - ntail:pad, :] = cache_ref[:, :, c * LANES:(c + 1) * LANES]


def _hist_append(ext_ref, j, pad, val, tail_ref):
    s = val.shape[0]
    ntail = tail_ref.shape[1]
    for c in range(ext_ref.shape[1]):
        ext_ref[j, c, pad:pad + s, :] = val[:, c * LANES:(c + 1) * LANES]
        tail_ref[j, :, c * LANES:(c + 1) * LANES] = ext_ref[j, c, pad + s - ntail:pad + s, :]


def _hist_carry(ext_ref, pad, s):
    ext_ref[0, :, 0:pad, :] = ext_ref[0, :, s:s + pad, :]


def _dwconv(ext_ref, j, first_row, rows, w_ref, b_ref, emit, rb):
    ntaps = w_ref.shape[0]
    for r0 in range(0, rows, rb):
        for c in range(ext_ref.shape[1]):
            cols = pl.ds(c * LANES, LANES)
            acc = b_ref[:, cols]
            for k in range(ntaps):
                acc = acc + ext_ref[j, c, pl.ds(first_row + k + r0, rb), :] * w_ref[pl.ds(k, 1), cols]
            emit(r0, c * LANES, acc)


def _segment_masks(row0, seg_rows):
    ti = lax.broadcasted_iota(jnp.int32, (SSD_BLOCK, SSD_BLOCK), 0)
    si = lax.broadcasted_iota(jnp.int32, (SSD_BLOCK, SSD_BLOCK), 1)
    if seg_rows >= SSD_BLOCK:
        same = None
        causal = ti >= si
    else:
        shift = seg_rows.bit_length() - 1
        same = ((ti + row0) >> shift) == ((si + row0) >> shift)
        causal = same & (ti >= si)
    return same, causal


def _ssd_block(c, seg_rows, xbcc_ref, dt_ref, alog_ref, dx_ref, e_ref, state_ref, y_ref):
    q = SSD_BLOCK
    assert seg_rows % q == 0 or q % seg_rows == 0
    row0 = c * q
    rows = pl.ds(row0, q)
    segs = range(row0 // seg_rows, (row0 + q - 1) // seg_rows + 1)
    multi = len(segs) > 1

    lane = lax.broadcasted_iota(jnp.int32, (1, LANES), 1)
    head_lane = lane < SSM_HEADS
    dt_lane = (lane >= SSM_HEADS) & (lane < 2 * SSM_HEADS)

    dt = dt_ref[rows, :]
    a_neg = jnp.where(head_lane, -jnp.exp(alog_ref[...]), 0.0)
    a = dt * a_neg

    same, causal = _segment_masks(row0, seg_rows)
    tri = jnp.where(causal, 1.0, 0.0).astype(BF16)
    acum = _dot_split_rhs(tri, a)
    if multi:
        seg_ones = jnp.where(same, 1.0, 0.0).astype(BF16)
        atot = _dot_split_rhs(seg_ones, a)
    else:
        atot = jnp.broadcast_to(acum[q - 1:q, :], (q, LANES))

    acum2 = acum * LOG2_E
    stacked_t = jnp.transpose(acum2 + jnp.where(dt_lane, dt, 0.0))

    e = _unpack(e_ref)
    ea_x = _dot_split_lhs(jnp.exp(acum), e)
    w_x = _dot_split_lhs(jnp.exp(atot - acum) * dt, e)
    seg_lo = [max(j * seg_rows, row0) - row0 for j in segs]
    first_rows = jnp.concatenate([atot[lo:lo + SUBLANES, :] for lo in seg_lo], axis=0)
    bd_x = _dot_split_lhs(jnp.exp(first_rows), e)

    head_cols = lax.broadcasted_iota(jnp.int32, (1, GROUP_W), 1) // SSM_HEAD_DIM
    if multi:
        row_seg = (lax.broadcasted_iota(jnp.int32, (q, 1), 0) + row0) // seg_rows

    for g in range(SSM_GROUPS):
        gcols = pl.ds(g * GROUP_W, GROUP_W)
        xs_g = xbcc_ref[rows, gcols]
        b_g = xbcc_ref[rows, pl.ds(D_SSM + g * D_STATE, D_STATE)]
        c_g = xbcc_ref[rows, pl.ds(D_SSM + SSM_GROUPS * D_STATE + g * D_STATE, D_STATE)]
        b_bf = b_g.astype(BF16)
        c_bf = c_g.astype(BF16)
        xs_bf = xs_g.astype(BF16)

        cb = lax.dot_general(c_bf, b_bf, (((1,), (1,)), ((), ())), preferred_element_type=F32)
        m_parts = []
        x_parts = []
        for hh in range(HEADS_PER_GROUP):
            h = g * HEADS_PER_GROUP + hh
            seg = acum2[:, h:h + 1] - stacked_t[h:h + 1, :]
            decay = jnp.exp2(jnp.where(causal, seg, -jnp.inf))
            m = cb * decay * stacked_t[SSM_HEADS + h:SSM_HEADS + h + 1, :]
            m_parts.append(m.astype(BF16))
            x_parts.append(jnp.where(head_cols == hh, xs_bf, jnp.zeros_like(xs_bf)))
        y_g = _dot(jnp.concatenate(m_parts, axis=1), jnp.concatenate(x_parts, axis=0))

        b_t = jnp.transpose(b_g).astype(BF16)
        xdec = xs_g * w_x[:, g * GROUP_W:(g + 1) * GROUP_W]

        off_parts = []
        for i, j in enumerate(segs):
            lo = max(j * seg_rows, row0) - row0
            hi = min((j + 1) * seg_rows, row0 + q) - row0
            h_prev = state_ref[j, :, gcols]
            off_parts.append(_dot(c_bf[lo:hi, :], h_prev.astype(BF16)))
            if multi:
                xm = jnp.where(row_seg == j, xdec, 0.0).astype(BF16)
            else:
                xm = xdec.astype(BF16)
            bd = bd_x[SUBLANES * i:SUBLANES * i + 1, g * GROUP_W:(g + 1) * GROUP_W]
            state_ref[j, :, gcols] = h_prev * bd + _dot(b_t, xm)
        y_off = off_parts[0] if len(off_parts) == 1 else jnp.concatenate(off_parts, axis=0)
        y_g = y_g + y_off * ea_x[:, g * GROUP_W:(g + 1) * GROUP_W] + dx_ref[:, gcols] * xs_g
        y_ref[rows, gcols] = y_g


def _mixer_kernel(*refs, n_seg, seg_rows, has_cache, n_steps):
    it = iter(refs)
    x_ref = next(it)
    if has_cache:
        ca_ref, cb_ref, st_ref = next(it), next(it), next(it)
    (nm_ref, wag_ref, wz_ref, wxbc_ref, wdt_ref, dtb_ref, alog_ref, caw_ref, cab_ref, lng_ref, lnb_ref,
     scw_ref, scb_ref, dx_ref, sn_ref, wout_ref, e_ref) = [next(it) for _ in range(17)]
    x1_ref, ta_ref, tb_ref, state_ref = [next(it) for _ in range(4)]
    h_ref, uext_ref, xext_ref, cva_ref, xbcc_ref, z_ref, dt_ref, y_ref, mix_ref = [next(it) for _ in range(9)]

    s = seg_rows
    t = pl.program_id(1)

    @pl.when(t == 0)
    def _():
        _hist_init(uext_ref, PAD_A, ca_ref if has_cache else None)
        _hist_init(xext_ref, PAD_B, cb_ref if has_cache else None)
        if has_cache:
            state_ref[...] = st_ref[...]
        else:
            state_ref[...] = jnp.zeros(state_ref.shape, F32)

    x = x_ref[...]
    h_ref[...] = _rmsnorm(x, nm_ref[...]).astype(BF16)
    hb = h_ref[...]

    ag = _dot(hb, _unpack(wag_ref))
    u = ag[:, :C_CONV] * _sigmoid(ag[:, C_CONV:])
    for j in range(n_seg):
        _hist_append(uext_ref, j, PAD_A, u[j * s:(j + 1) * s, :], ta_ref)

        def emit_a(r0, c0, v, j=j):
            cva_ref[pl.ds(j * s + r0, v.shape[0]), pl.ds(c0, v.shape[1])] = v

        _dwconv(uext_ref, j, PAD_A - (CONV_W - 1), s, caw_ref, cab_ref, emit_a, rb=min(s, CONV_ROWS))
    uc = cva_ref[...]
    mu = jnp.mean(uc, axis=-1, keepdims=True)
    xc = uc - mu
    var = jnp.mean(xc * xc, axis=-1, keepdims=True)
    ln = xc * lax.rsqrt(var + EPS) * lng_ref[...] + lnb_ref[...]
    mix_ref[:, 0:C_CONV] = _silu(ln).astype(BF16)

    z_ref[...] = _dot(hb, _unpack(wz_ref))
    xbc = _dot(hb, _unpack(wxbc_ref))
    for j in range(n_seg):
        _hist_append(xext_ref, j, PAD_B, xbc[j * s:(j + 1) * s, :], tb_ref)

        def emit_b(r0, c0, v, j=j):
            xbcc_ref[pl.ds(j * s + r0, v.shape[0]), pl.ds(c0, v.shape[1])] = _silu(v)

        _dwconv(xext_ref, j, PAD_B - (SSM_CONV_W - 1), s, scw_ref, scb_ref, emit_b, rb=min(s, CONV_ROWS))
    dt_ref[...] = _softplus(_dot(hb, _unpack(wdt_ref)) + dtb_ref[...])
    for c in range(n_seg * s // SSD_BLOCK):
        _ssd_block(c, s, xbcc_ref, dt_ref, alog_ref, dx_ref, e_ref, state_ref, y_ref)
    gated = y_ref[...] * _silu(z_ref[...])
    mix_ref[:, C_CONV:] = _rmsnorm(gated, sn_ref[...]).astype(BF16)

    x1_ref[...] = x + _dot(mix_ref[...], _unpack(wout_ref))

    if n_steps > 1:
        _hist_carry(uext_ref, PAD_A, s)
        _hist_carry(xext_ref, PAD_B, s)


def _ffn_kernel(*refs, n_seg, seg_rows, has_cache, n_steps, final_norm):
    it = iter(refs)
    x1_ref = next(it)
    if has_cache:
        cf_ref = next(it)
    nf_ref, wg_ref, wv_ref, fcw_ref, fcb_ref, wd_ref = [next(it) for _ in range(6)]
    if final_norm:
        nfin_ref = next(it)
    y_ref, tf_ref = next(it), next(it)
    hf_ref, gext_ref, val_ref, act_ref = [next(it) for _ in range(4)]

    s = seg_rows
    t = pl.program_id(1)

    @pl.when(t == 0)
    def _():
        _hist_init(gext_ref, PAD_F, cf_ref if has_cache else None)

    x1 = x1_ref[...]
    hf_ref[...] = _rmsnorm(x1, nf_ref[...]).astype(BF16)
    hf = hf_ref[...]
    gate = _dot(hf, _unpack(wg_ref))
    val_ref[...] = _dot(hf, _unpack(wv_ref))
    for j in range(n_seg):
        _hist_append(gext_ref, j, PAD_F, gate[j * s:(j + 1) * s, :], tf_ref)

        def emit_f(r0, c0, v, j=j):
            rows = pl.ds(j * s + r0, v.shape[0])
            cols = pl.ds(c0, v.shape[1])
            act_ref[rows, cols] = (_silu(v) * val_ref[rows, cols]).astype(BF16)

        _dwconv(gext_ref, j, PAD_F - (FFN_CONV_W - 1), s, fcw_ref, fcb_ref, emit_f, rb=min(s, CONV_ROWS))
    x2 = x1 + _dot(act_ref[...], _unpack(wd_ref))
    if final_norm:
        x2 = _rmsnorm(x2, nfin_ref[...])
    y_ref[...] = x2

    if n_steps > 1:
        _hist_carry(gext_ref, PAD_F, s)


def _const_spec(arr):
    nd = arr.ndim
    return pl.BlockSpec(arr.shape, lambda b, t, _nd=nd: (0,) * _nd, pipeline_mode=pl.Buffered(1))


def _layout(n_streams, length):
    if length >= TILE:
        assert length % TILE == 0
        return TILE, 1, TILE, (n_streams, length // TILE)
    tile = SSD_BLOCK
    assert tile % length == 0 and (n_streams * length) % tile == 0 and length % SUBLANES == 0
    n_seg = tile // length
    return tile, n_seg, length, (n_streams // n_seg, 1)


def _mixer(x, caches, p):
    n_streams, length, _ = x.shape
    tile, n_seg, s, grid = _layout(n_streams, length)
    has_cache = caches is not None
    xt = x.reshape(grid[0], grid[1] * tile, D_MODEL)

    tile_spec = pl.BlockSpec((None, tile, D_MODEL), lambda b, t: (b, t, 0))

    def per_batch(shape, **kw):
        return pl.BlockSpec((n_seg,) + shape, lambda b, t: (b, 0, 0), **kw)

    inputs = [xt]
    in_specs = [tile_spec]
    if has_cache:
        inputs += list(caches)
        once = dict(pipeline_mode=pl.Buffered(1))
        in_specs += [per_batch((CONV_W - 1, C_CONV), **once), per_batch((SSM_CONV_W - 1, D_XBC), **once),
                     per_batch((D_STATE, D_SSM), **once)]
    consts = [p["norm_mix"], p["w_ag"], p["w_z"], p["w_xbc"], p["w_dt"], p["dt_bias"], p["a_log"],
              p["conv_a_w"], p["conv_a_b"], p["ln_a_g"], p["ln_a_b"], p["ssm_conv_w"], p["ssm_conv_b"],
              p["d_x"], p["ssm_norm"], p["w_out"], p["expand"]]
    inputs += consts
    in_specs += [_const_spec(c) for c in consts]

    out_shape = (
        jax.ShapeDtypeStruct(xt.shape, F32),
        jax.ShapeDtypeStruct((n_streams, CONV_W - 1, C_CONV), F32),
        jax.ShapeDtypeStruct((n_streams, SSM_CONV_W - 1, D_XBC), F32),
        jax.ShapeDtypeStruct((n_streams, D_STATE, D_SSM), F32),
    )
    out_specs = (tile_spec, per_batch((CONV_W - 1, C_CONV)), per_batch((SSM_CONV_W - 1, D_XBC)),
                 per_batch((D_STATE, D_SSM)))
    scratch = [
        pltpu.VMEM((tile, D_MODEL), BF16),
        pltpu.VMEM((n_seg, C_CONV // LANES, PAD_A + s, LANES), F32),
        pltpu.VMEM((n_seg, D_XBC // LANES, PAD_B + s, LANES), F32),
        pltpu.VMEM((tile, C_CONV), F32),
        pltpu.VMEM((tile, D_XBC), F32),
        pltpu.VMEM((tile, D_SSM), F32),
        pltpu.VMEM((tile, LANES), F32),
        pltpu.VMEM((tile, D_SSM), F32),
        pltpu.VMEM((tile, C_CONV + D_SSM), BF16),
    ]
    kern = functools.partial(_mixer_kernel, n_seg=n_seg, seg_rows=s, has_cache=has_cache, n_steps=grid[1])
    x1, ta, tb, st = pl.pallas_call(
        kern, grid=grid, in_specs=in_specs, out_specs=out_specs, out_shape=out_shape,
        scratch_shapes=scratch, name="mixer_cache" if has_cache else "mixer_fresh",
        compiler_params=pltpu.CompilerParams(
            dimension_semantics=("arbitrary", "arbitrary"), vmem_limit_bytes=VMEM_LIMIT_BYTES),
    )(*inputs)
    return x1.reshape(x.shape), ta, tb, st


def _ffn(x1, cache, p, norm_final):
    n_streams, length, _ = x1.shape
    tile, n_seg, s, grid = _layout(n_streams, length)
    has_cache = cache is not None
    xt = x1.reshape(grid[0], grid[1] * tile, D_MODEL)
    tile_spec = pl.BlockSpec((None, tile, D_MODEL), lambda b, t: (b, t, 0))
    tail_spec = pl.BlockSpec((n_seg, FFN_CONV_W - 1, D_FF), lambda b, t: (b, 0, 0))

    inputs = [xt]
    in_specs = [tile_spec]
    if has_cache:
        inputs.append(cache)
        in_specs.append(tail_spec)
    consts = [p["norm_ffn"], p["w_gate"], p["w_val"], p["ffn_conv_w"], p["ffn_conv_b"], p["w_down"]]
    if norm_final is not None:
        consts.append(norm_final)
    inputs += consts
    in_specs += [_const_spec(c) for c in consts]

    out_shape = (jax.ShapeDtypeStruct(xt.shape, F32),
                 jax.ShapeDtypeStruct((n_streams, FFN_CONV_W - 1, D_FF), F32))
    scratch = [
        pltpu.VMEM((tile, D_MODEL), BF16),
        pltpu.VMEM((n_seg, D_FF // LANES, PAD_F + s, LANES), F32),
        pltpu.VMEM((tile, D_FF), F32),
        pltpu.VMEM((tile, D_FF), BF16),
    ]
    kern = functools.partial(_ffn_kernel, n_seg=n_seg, seg_rows=s, has_cache=has_cache, n_steps=grid[1],
                             final_norm=norm_final is not None)
    y, tf = pl.pallas_call(
        kern, grid=grid, in_specs=in_specs, out_specs=(tile_spec, tail_spec), out_shape=out_shape,
        scratch_shapes=scratch, name="ffn_cache" if has_cache else "ffn_fresh",
        compiler_params=pltpu.CompilerParams(
            dimension_semantics=("arbitrary", "arbitrary"), vmem_limit_bytes=VMEM_LIMIT_BYTES),
    )(*inputs)
    return y.reshape(x1.shape), tf


def _pad_lanes(v, reps=1):
    v = jnp.tile(v.astype(F32), reps)
    return jnp.pad(v, (0, LANES - v.shape[0]))[None, :]


def _pack_bf16(w):
    b = lax.bitcast_convert_type(w.astype(BF16), jnp.uint16).astype(jnp.uint32)
    return b[0::2] | (b[1::2] << 16)


def _layer_params(norm_mix, w_in, conv_a_w, conv_a_b, ln_a_g, ln_a_b, ssm_conv_w, ssm_conv_b, dt_bias, a_log,
                  d_skip, ssm_norm, w_out, norm_ffn, w_up, ffn_conv_w, ffn_conv_b, w_down):
    o_z = 2 * C_CONV
    o_xbc = o_z + D_SSM
    o_dt = o_xbc + D_XBC
    w_dt = w_in[:, o_dt:o_dt + SSM_HEADS]
    w_dt = jnp.pad(jnp.concatenate([w_dt, w_dt], axis=1), ((0, 0), (0, LANES - 2 * SSM_HEADS)))
    head_of_col = jnp.arange(D_SSM, dtype=jnp.int32) // SSM_HEAD_DIM
    expand = (jnp.arange(LANES, dtype=jnp.int32)[:, None] == head_of_col[None, :]).astype(F32)
    return {
        "norm_mix": norm_mix[None, :],
        "w_ag": _pack_bf16(w_in[:, :o_z]),
        "w_z": _pack_bf16(w_in[:, o_z:o_xbc]),
        "w_xbc": _pack_bf16(w_in[:, o_xbc:o_dt]),
        "w_dt": _pack_bf16(w_dt),
        "dt_bias": _pad_lanes(dt_bias, 2),
        "a_log": _pad_lanes(a_log),
        "conv_a_w": conv_a_w, "conv_a_b": conv_a_b[None, :],
        "ln_a_g": ln_a_g[None, :], "ln_a_b": ln_a_b[None, :],
        "ssm_conv_w": ssm_conv_w, "ssm_conv_b": ssm_conv_b[None, :],
        "d_x": jnp.repeat(d_skip, SSM_HEAD_DIM)[None, :],
        "ssm_norm": ssm_norm[None, :],
        "w_out": _pack_bf16(w_out),
        "expand": _pack_bf16(expand),
        "norm_ffn": norm_ffn[None, :],
        "w_gate": _pack_bf16(w_up[:, :D_FF]),
        "w_val": _pack_bf16(w_up[:, D_FF:]),
        "ffn_conv_w": ffn_conv_w, "ffn_conv_b": ffn_conv_b[None, :],
        "w_down": _pack_bf16(w_down),
    }


def _state_to_kernel(st):
    n = st.shape[0]
    return jnp.transpose(st, (0, 3, 1, 2)).reshape(n, D_STATE, D_SSM)


def _state_from_kernel(st):
    n = st.shape[0]
    return jnp.transpose(st.reshape(n, D_STATE, SSM_HEADS, SSM_HEAD_DIM), (0, 2, 3, 1))


def kernel(x_prompt, x_sample, cache_conv_a, cache_ssm_conv, state_ssm, cache_ffn_conv, norm_mix, w_in, conv_a_w,
           conv_a_b, ln_a_g, ln_a_b, ssm_conv_w, ssm_conv_b, dt_bias, a_log, d_skip, ssm_norm, w_out, norm_ffn,
           w_up, ffn_conv_w, ffn_conv_b, w_down, norm_final):
    depth = w_in.shape[0]
    xp, xs = x_prompt, x_sample
    outs = [[] for _ in range(8)]
    for i in range(depth):
        p = _layer_params(norm_mix[i], w_in[i], conv_a_w[i], conv_a_b[i], ln_a_g[i], ln_a_b[i], ssm_conv_w[i],
                          ssm_conv_b[i], dt_bias[i], a_log[i], d_skip[i], ssm_norm[i], w_out[i], norm_ffn[i],
                          w_up[i], ffn_conv_w[i], ffn_conv_b[i], w_down[i])
        nfin = norm_final[None, :] if i == depth - 1 else None
        xp, pa, pb, ph = _mixer(xp, None, p)
        xp, pf = _ffn(xp, None, p, nfin)
        xs, sa, sb, sh = _mixer(xs, (cache_conv_a[i], cache_ssm_conv[i], _state_to_kernel(state_ssm[i])), p)
        xs, sf = _ffn(xs, cache_ffn_conv[i], p, nfin)
        for lst, v in zip(outs, (pa, pb, _state_from_kernel(ph), pf, sa, sb, _state_from_kernel(sh), sf)):
            lst.append(v)
    return (xp, xs) + tuple(jnp.stack(o) for o in outs)
```

```python
import functools

import jax
import jax.numpy as jnp
from jax import lax
from jax.experimental import pallas as pl
from jax.experimental.pallas import tpu as pltpu

D_MODEL = 1024
C_CONV = 1024
CONV_W = 31
SSM_HEADS = 16
SSM_HEAD_DIM = 64
D_SSM = SSM_HEADS * SSM_HEAD_DIM
SSM_GROUPS = 4
HEADS_PER_GROUP = SSM_HEADS // SSM_GROUPS
D_STATE = 128
SSM_CONV_W = 4
D_XBC = D_SSM + 2 * SSM_GROUPS * D_STATE
D_FF = 2816
FFN_CONV_W = 3
EPS = 1e-6
O_Z = 2 * C_CONV
O_XBC = O_Z + D_SSM
O_DT = O_XBC + D_XBC
LOG2_E = 1.4426950408889634

SUBLANES = 8
LANES = 128
MIXER_TILE = 256
FFN_TILE = 512
SSD_BLOCK = LANES
PACK_COLS = 512
CONV_ROWS = 128
PAD_A = 32
PAD_B = 8
PAD_F = 8
GROUP_W = HEADS_PER_GROUP * SSM_HEAD_DIM
VMEM_LIMIT_BYTES = 56 * 1024 * 1024

F32 = jnp.float32
BF16 = jnp.bfloat16


def _sigmoid(x):
    return jax.nn.sigmoid(x)


def _silu(x):
    return x * _sigmoid(x)


def _rmsnorm(x, g):
    return x * lax.rsqrt(jnp.mean(x * x, axis=-1, keepdims=True) + EPS) * g


def _softplus(x):
    return jnp.maximum(x, 0.0) + jnp.log1p(jnp.exp(-jnp.abs(x)))


def _split_bf16(v):
    hi = v.astype(BF16)
    lo = (v - hi.astype(F32)).astype(BF16)
    return hi, lo


def _dot(a, b):
    return jnp.dot(a, b, preferred_element_type=F32)


def _dot_split_lhs(v, m):
    hi, lo = _split_bf16(v)
    return _dot(hi, m) + _dot(lo, m)


def _dot_split_rhs(m, v):
    hi, lo = _split_bf16(v)
    return _dot(m, hi) + _dot(m, lo)


def _unpack(wp_ref, lo=0, hi=None):
    hi = wp_ref.shape[1] if hi is None else hi
    return pltpu.bitcast(wp_ref[:, lo:hi], BF16)


def _hist_init(ext_ref, pad, cache_ref):
    n_seg, n_slab = ext_ref.shape[0], ext_ref.shape[1]
    ext_ref[:, :, 0:pad, :] = jnp.zeros((n_seg, n_slab, pad, LANES), F32)
    if cache_ref is not None:
        ntail = cache_ref.shape[1]
        for c in range(n_slab):
            ext_ref[:, c, pad - ntail:pad, :] = cache_ref[:, :, c * LANES:(c + 1) * LANES]


def _hist_append(ext_ref, j, pad, val, tail_ref):
    s = val.shape[0]
    ntail = tail_ref.shape[1]
    for c in range(ext_ref.shape[1]):
        ext_ref[j, c, pad:pad + s, :] = val[:, c * LANES:(c + 1) * LANES]
        tail_ref[j, :, c * LANES:(c + 1) * LANES] = ext_ref[j, c, pad + s - ntail:pad + s, :]


def _hist_carry(ext_ref, pad, s):
    ext_ref[0, :, 0:pad, :] = ext_ref[0, :, s:s + pad, :]


def _dwconv(ext_ref, j, first_row, rows, w_ref, b_ref, emit, rb):
    ntaps = w_ref.shape[0]
    for r0 in range(0, rows, rb):
        for c in range(ext_ref.shape[1]):
            cols = pl.ds(c * LANES, LANES)
            acc = b_ref[:, cols]
            for k in range(ntaps):
                acc = acc + ext_ref[j, c, pl.ds(first_row + k + r0, rb), :] * w_ref[pl.ds(k, 1), cols]
            emit(r0, c * LANES, acc)


def _segment_masks(row0, seg_rows):
    ti = lax.broadcasted_iota(jnp.int32, (SSD_BLOCK, SSD_BLOCK), 0)
    si = lax.broadcasted_iota(jnp.int32, (SSD_BLOCK, SSD_BLOCK), 1)
    if seg_rows >= SSD_BLOCK:
        same = None
        causal = ti >= si
    else:
        shift = seg_rows.bit_length() - 1
        same = ((ti + row0) >> shift) == ((si + row0) >> shift)
        causal = same & (ti >= si)
    return same, causal


def _ssd_block(c, seg_rows, xbcc_ref, dt_ref, alog_ref, dx_ref, e_ref, state_ref, y_ref):
    q = SSD_BLOCK
    assert seg_rows % q == 0 or q % seg_rows == 0
    row0 = c * q
    rows = pl.ds(row0, q)
    segs = range(row0 // seg_rows, (row0 + q - 1) // seg_rows + 1)
    multi = len(segs) > 1

    lane = lax.broadcasted_iota(jnp.int32, (1, LANES), 1)
    head_lane = lane < SSM_HEADS
    dt_lane = (lane >= SSM_HEADS) & (lane < 2 * SSM_HEADS)

    dt = dt_ref[rows, :]
    a_neg = jnp.where(head_lane, -jnp.exp(alog_ref[...]), 0.0)
    a = dt * a_neg

    same, causal = _segment_masks(row0, seg_rows)
    tri = jnp.where(causal, 1.0, 0.0).astype(BF16)
    acum = _dot_split_rhs(tri, a)
    if multi:
        seg_ones = jnp.where(same, 1.0, 0.0).astype(BF16)
        atot = _dot_split_rhs(seg_ones, a)
    else:
        atot = jnp.broadcast_to(acum[q - 1:q, :], (q, LANES))

    acum2 = acum * LOG2_E
    stacked_t = jnp.transpose(acum2 + jnp.where(dt_lane, dt, 0.0))

    e = _unpack(e_ref)
    ea_x = _dot_split_lhs(jnp.exp(acum), e)
    w_x = _dot_split_lhs(jnp.exp(atot - acum) * dt, e)
    seg_lo = [max(j * seg_rows, row0) - row0 for j in segs]
    first_rows = jnp.concatenate([atot[lo:lo + SUBLANES, :] for lo in seg_lo], axis=0)
    bd_x = _dot_split_lhs(jnp.exp(first_rows), e)

    head_cols = lax.broadcasted_iota(jnp.int32, (1, GROUP_W), 1) // SSM_HEAD_DIM
    if multi:
        row_seg = (lax.broadcasted_iota(jnp.int32, (q, 1), 0) + row0) // seg_rows

    for g in range(SSM_GROUPS):
        gcols = pl.ds(g * GROUP_W, GROUP_W)
        xs_g = xbcc_ref[rows, gcols]
        b_g = xbcc_ref[rows, pl.ds(D_SSM + g * D_STATE, D_STATE)]
        c_g = xbcc_ref[rows, pl.ds(D_SSM + SSM_GROUPS * D_STATE + g * D_STATE, D_STATE)]
        b_bf = b_g.astype(BF16)
        c_bf = c_g.astype(BF16)
        xs_bf = xs_g.astype(BF16)

        cb = lax.dot_general(c_bf, b_bf, (((1,), (1,)), ((), ())), preferred_element_type=F32)
        m_parts = []
        x_parts = []
        for hh in range(HEADS_PER_GROUP):
            h = g * HEADS_PER_GROUP + hh
            seg = acum2[:, h:h + 1] - stacked_t[h:h + 1, :]
            decay = jnp.exp2(jnp.where(causal, seg, -jnp.inf))
            m = cb * decay * stacked_t[SSM_HEADS + h:SSM_HEADS + h + 1, :]
            m_parts.append(m.astype(BF16))
            x_parts.append(jnp.where(head_cols == hh, xs_bf, jnp.zeros_like(xs_bf)))
        y_g = _dot(jnp.concatenate(m_parts, axis=1), jnp.concatenate(x_parts, axis=0))

        b_t = jnp.transpose(b_g).astype(BF16)
        xdec = xs_g * w_x[:, g * GROUP_W:(g + 1) * GROUP_W]

        off_parts = []
        for i, j in enumerate(segs):
            lo = max(j * seg_rows, row0) - row0
            hi = min((j + 1) * seg_rows, row0 + q) - row0
            h_prev = state_ref[j, :, gcols]
            off_parts.append(_dot(c_bf[lo:hi, :], h_prev.astype(BF16)))
            if multi:
                xm = jnp.where(row_seg == j, xdec, 0.0).astype(BF16)
            else:
                xm = xdec.astype(BF16)
            bd = bd_x[SUBLANES * i:SUBLANES * i + 1, g * GROUP_W:(g + 1) * GROUP_W]
            state_ref[j, :, gcols] = h_prev * bd + _dot(b_t, xm)
        y_off = off_parts[0] if len(off_parts) == 1 else jnp.concatenate(off_parts, axis=0)
        y_g = y_g + y_off * ea_x[:, g * GROUP_W:(g + 1) * GROUP_W] + dx_ref[:, gcols] * xs_g
        y_ref[rows, gcols] = y_g


def _mixer_kernel(*refs, n_seg, seg_rows, has_cache, n_steps):
    it = iter(refs)
    x_ref = next(it)
    if has_cache:
        ca_ref, cb_ref, st_ref = next(it), next(it), next(it)
    (nm_ref, win_ref, wdt_ref, dtb_ref, alog_ref, caw_ref, cab_ref, lng_ref, lnb_ref,
     scw_ref, scb_ref, dx_ref, sn_ref, wout_ref, e_ref) = [next(it) for _ in range(15)]
    x1_ref, ta_ref, tb_ref, state_ref = [next(it) for _ in range(4)]
    h_ref, uext_ref, xext_ref, cva_ref, xbcc_ref, z_ref, dt_ref, y_ref, mix_ref = [next(it) for _ in range(9)]

    s = seg_rows
    t = pl.program_id(1)

    @pl.when(t == 0)
    def _():
        _hist_init(uext_ref, PAD_A, ca_ref if has_cache else None)
        _hist_init(xext_ref, PAD_B, cb_ref if has_cache else None)
        if has_cache:
            state_ref[...] = st_ref[...]
        else:
            state_ref[...] = jnp.zeros(state_ref.shape, F32)

    x = x_ref[...]
    h_ref[...] = _rmsnorm(x, nm_ref[...]).astype(BF16)
    hb = h_ref[...]

    ag = _dot(hb, _unpack(win_ref, 0, O_Z))
    u = ag[:, :C_CONV] * _sigmoid(ag[:, C_CONV:])
    for j in range(n_seg):
        _hist_append(uext_ref, j, PAD_A, u[j * s:(j + 1) * s, :], ta_ref)

        def emit_a(r0, c0, v, j=j):
            cva_ref[pl.ds(j * s + r0, v.shape[0]), pl.ds(c0, v.shape[1])] = v

        _dwconv(uext_ref, j, PAD_A - (CONV_W - 1), s, caw_ref, cab_ref, emit_a, rb=min(s, CONV_ROWS))
    uc = cva_ref[...]
    mu = jnp.mean(uc, axis=-1, keepdims=True)
    xc = uc - mu
    var = jnp.mean(xc * xc, axis=-1, keepdims=True)
    ln = xc * lax.rsqrt(var + EPS) * lng_ref[...] + lnb_ref[...]
    mix_ref[:, 0:C_CONV] = _silu(ln).astype(BF16)

    z_ref[...] = _dot(hb, _unpack(win_ref, O_Z, O_XBC))
    xbc = _dot(hb, _unpack(win_ref, O_XBC, O_DT))
    for j in range(n_seg):
        _hist_append(xext_ref, j, PAD_B, xbc[j * s:(j + 1) * s, :], tb_ref)

        def emit_b(r0, c0, v, j=j):
            xbcc_ref[pl.ds(j * s + r0, v.shape[0]), pl.ds(c0, v.shape[1])] = _silu(v)

        _dwconv(xext_ref, j, PAD_B - (SSM_CONV_W - 1), s, scw_ref, scb_ref, emit_b, rb=min(s, CONV_ROWS))
    dt_ref[...] = _softplus(_dot(hb, _unpack(wdt_ref)) + dtb_ref[...])
    for c in range(n_seg * s // SSD_BLOCK):
        _ssd_block(c, s, xbcc_ref, dt_ref, alog_ref, dx_ref, e_ref, state_ref, y_ref)
    gated = y_ref[...] * _silu(z_ref[...])
    mix_ref[:, C_CONV:] = _rmsnorm(gated, sn_ref[...]).astype(BF16)

    x1_ref[...] = x + _dot(mix_ref[...], _unpack(wout_ref))

    if n_steps > 1:
        _hist_carry(uext_ref, PAD_A, s)
        _hist_carry(xext_ref, PAD_B, s)


def _ffn_kernel(*refs, n_seg, seg_rows, has_cache, n_steps, final_norm):
    it = iter(refs)
    x1_ref = next(it)
    if has_cache:
        cf_ref = next(it)
    nf_ref, wup_ref, fcw_ref, fcb_ref, wd_ref = [next(it) for _ in range(5)]
    if final_norm:
        nfin_ref = next(it)
    y_ref, tf_ref = next(it), next(it)
    hf_ref, gext_ref, val_ref, act_ref = [next(it) for _ in range(4)]

    s = seg_rows
    t = pl.program_id(1)

    @pl.when(t == 0)
    def _():
        _hist_init(gext_ref, PAD_F, cf_ref if has_cache else None)

    x1 = x1_ref[...]
    hf_ref[...] = _rmsnorm(x1, nf_ref[...]).astype(BF16)
    hf = hf_ref[...]
    gate = _dot(hf, _unpack(wup_ref, 0, D_FF))
    val_ref[...] = _dot(hf, _unpack(wup_ref, D_FF, 2 * D_FF))
    for j in range(n_seg):
        _hist_append(gext_ref, j, PAD_F, gate[j * s:(j + 1) * s, :], tf_ref)

        def emit_f(r0, c0, v, j=j):
            rows = pl.ds(j * s + r0, v.shape[0])
            cols = pl.ds(c0, v.shape[1])
            act_ref[rows, cols] = (_silu(v) * val_ref[rows, cols]).astype(BF16)

        _dwconv(gext_ref, j, PAD_F - (FFN_CONV_W - 1), s, fcw_ref, fcb_ref, emit_f, rb=min(s, CONV_ROWS))
    x2 = x1 + _dot(act_ref[...], _unpack(wd_ref))
    if final_norm:
        x2 = _rmsnorm(x2, nfin_ref[...])
    y_ref[...] = x2

    if n_steps > 1:
        _hist_carry(gext_ref, PAD_F, s)


def _const_spec(arr):
    nd = arr.ndim
    return pl.BlockSpec(arr.shape, lambda b, t, _nd=nd: (0,) * _nd, pipeline_mode=pl.Buffered(1))


def _layout(n_streams, length, long_tile):
    if length >= long_tile:
        assert length % long_tile == 0
        return long_tile, 1, long_tile, (n_streams, length // long_tile)
    tile = SSD_BLOCK
    assert tile % length == 0 and (n_streams * length) % tile == 0 and length % SUBLANES == 0
    n_seg = tile // length
    return tile, n_seg, length, (n_streams // n_seg, 1)


def _mixer(x, caches, p):
    n_streams, length, _ = x.shape
    tile, n_seg, s, grid = _layout(n_streams, length, MIXER_TILE)
    has_cache = caches is not None
    xt = x.reshape(grid[0], grid[1] * tile, D_MODEL)

    tile_spec = pl.BlockSpec((None, tile, D_MODEL), lambda b, t: (b, t, 0))

    def per_batch(shape, **kw):
        return pl.BlockSpec((n_seg,) + shape, lambda b, t: (b, 0, 0), **kw)

    inputs = [xt]
    in_specs = [tile_spec]
    if has_cache:
        inputs += list(caches)
        once = dict(pipeline_mode=pl.Buffered(1))
        in_specs += [per_batch((CONV_W - 1, C_CONV), **once), per_batch((SSM_CONV_W - 1, D_XBC), **once),
                     per_batch((D_STATE, D_SSM), **once)]
    consts = [p["norm_mix"], p["w_in"], p["w_dt"], p["dt_bias"], p["a_log"],
              p["conv_a_w"], p["conv_a_b"], p["ln_a_g"], p["ln_a_b"], p["ssm_conv_w"], p["ssm_conv_b"],
              p["d_x"], p["ssm_norm"], p["w_out"], p["expand"]]
    inputs += consts
    in_specs += [_const_spec(c) for c in consts]

    out_shape = (
        jax.ShapeDtypeStruct(xt.shape, F32),
        jax.ShapeDtypeStruct((n_streams, CONV_W - 1, C_CONV), F32),
        jax.ShapeDtypeStruct((n_streams, SSM_CONV_W - 1, D_XBC), F32),
        jax.ShapeDtypeStruct((n_streams, D_STATE, D_SSM), F32),
    )
    out_specs = (tile_spec, per_batch((CONV_W - 1, C_CONV)), per_batch((SSM_CONV_W - 1, D_XBC)),
                 per_batch((D_STATE, D_SSM)))
    scratch = [
        pltpu.VMEM((tile, D_MODEL), BF16),
        pltpu.VMEM((n_seg, C_CONV // LANES, PAD_A + s, LANES), F32),
        pltpu.VMEM((n_seg, D_XBC // LANES, PAD_B + s, LANES), F32),
        pltpu.VMEM((tile, C_CONV), F32),
        pltpu.VMEM((tile, D_XBC), F32),
        pltpu.VMEM((tile, D_SSM), F32),
        pltpu.VMEM((tile, LANES), F32),
        pltpu.VMEM((tile, D_SSM), F32),
        pltpu.VMEM((tile, C_CONV + D_SSM), BF16),
    ]
    kern = functools.partial(_mixer_kernel, n_seg=n_seg, seg_rows=s, has_cache=has_cache, n_steps=grid[1])
    x1, ta, tb, st = pl.pallas_call(
        kern, grid=grid, in_specs=in_specs, out_specs=out_specs, out_shape=out_shape,
        scratch_shapes=scratch, name="mixer_cache" if has_cache else "mixer_fresh",
        compiler_params=pltpu.CompilerParams(
            dimension_semantics=("arbitrary", "arbitrary"), vmem_limit_bytes=VMEM_LIMIT_BYTES),
    )(*inputs)
    return x1.reshape(x.shape), ta, tb, st


def _ffn(x1, cache, p, norm_final):
    n_streams, length, _ = x1.shape
    tile, n_seg, s, grid = _layout(n_streams, length, FFN_TILE)
    has_cache = cache is not None
    xt = x1.reshape(grid[0], grid[1] * tile, D_MODEL)
    tile_spec = pl.BlockSpec((None, tile, D_MODEL), lambda b, t: (b, t, 0))
    tail_spec = pl.BlockSpec((n_seg, FFN_CONV_W - 1, D_FF), lambda b, t: (b, 0, 0))

    inputs = [xt]
    in_specs = [tile_spec]
    if has_cache:
        inputs.append(cache)
        in_specs.append(tail_spec)
    consts = [p["norm_ffn"], p["w_up"], p["ffn_conv_w"], p["ffn_conv_b"], p["w_down"]]
    if norm_final is not None:
        consts.append(norm_final)
    inputs += consts
    in_specs += [_const_spec(c) for c in consts]

    out_shape = (jax.ShapeDtypeStruct(xt.shape, F32),
                 jax.ShapeDtypeStruct((n_streams, FFN_CONV_W - 1, D_FF), F32))
    scratch = [
        pltpu.VMEM((tile, D_MODEL), BF16),
        pltpu.VMEM((n_seg, D_FF // LANES, PAD_F + s, LANES), F32),
        pltpu.VMEM((tile, D_FF), F32),
        pltpu.VMEM((tile, D_FF), BF16),
    ]
    kern = functools.partial(_ffn_kernel, n_seg=n_seg, seg_rows=s, has_cache=has_cache, n_steps=grid[1],
                             final_norm=norm_final is not None)
    y, tf = pl.pallas_call(
        kern, grid=grid, in_specs=in_specs, out_specs=(tile_spec, tail_spec), out_shape=out_shape,
        scratch_shapes=scratch, name="ffn_cache" if has_cache else "ffn_fresh",
        compiler_params=pltpu.CompilerParams(
            dimension_semantics=("arbitrary", "arbitrary"), vmem_limit_bytes=VMEM_LIMIT_BYTES),
    )(*inputs)
    return y.reshape(x1.shape), tf


def _pad_lanes(v, reps=1):
    v = jnp.tile(v.astype(F32), reps)
    return jnp.pad(v, (0, LANES - v.shape[0]))[None, :]


def _pack_kernel(w_ref, o_ref):
    o_ref[...] = pltpu.bitcast(w_ref[...].astype(BF16), jnp.uint32)


def _pack_bf16(w, n_cols=None):
    k, n = w.shape
    n_cols = n if n_cols is None else n_cols
    bn = min(PACK_COLS, n_cols)
    assert k % (2 * SUBLANES) == 0 and n_cols % bn == 0
    return pl.pallas_call(
        _pack_kernel, grid=(n_cols // bn,),
        in_specs=[pl.BlockSpec((k, bn), lambda c: (0, c))],
        out_specs=pl.BlockSpec((k // 2, bn), lambda c: (0, c)),
        out_shape=jax.ShapeDtypeStruct((k // 2, n_cols), jnp.uint32), name="pack_weights",
        compiler_params=pltpu.CompilerParams(dimension_semantics=("arbitrary",)),
    )(w)


def _layer_params(norm_mix, w_in, conv_a_w, conv_a_b, ln_a_g, ln_a_b, ssm_conv_w, ssm_conv_b, dt_bias, a_log,
                  d_skip, ssm_norm, w_out, norm_ffn, w_up, ffn_conv_w, ffn_conv_b, w_down):
    w_dt = w_in[:, O_DT:O_DT + SSM_HEADS]
    w_dt = jnp.pad(jnp.concatenate([w_dt, w_dt], axis=1), ((0, 0), (0, LANES - 2 * SSM_HEADS)))
    head_of_col = jnp.arange(D_SSM, dtype=jnp.int32) // SSM_HEAD_DIM
    expand = (jnp.arange(LANES, dtype=jnp.int32)[:, None] == head_of_col[None, :]).astype(F32)
    return {
        "norm_mix": norm_mix[None, :],
        "w_in": _pack_bf16(w_in, O_DT),
        "w_dt": _pack_bf16(w_dt),
        "dt_bias": _pad_lanes(dt_bias, 2),
        "a_log": _pad_lanes(a_log),
        "conv_a_w": conv_a_w, "conv_a_b": conv_a_b[None, :],
        "ln_a_g": ln_a_g[None, :], "ln_a_b": ln_a_b[None, :],
        "ssm_conv_w": ssm_conv_w, "ssm_conv_b": ssm_conv_b[None, :],
        "d_x": jnp.repeat(d_skip, SSM_HEAD_DIM)[None, :],
        "ssm_norm": ssm_norm[None, :],
        "w_out": _pack_bf16(w_out),
        "expand": _pack_bf16(expand),
        "norm_ffn": norm_ffn[None, :],
        "w_up": _pack_bf16(w_up),
        "ffn_conv_w": ffn_conv_w, "ffn_conv_b": ffn_conv_b[None, :],
        "w_down": _pack_bf16(w_down),
    }


def _state_to_kernel(st):
    n = st.shape[0]
    return jnp.transpose(st, (0, 3, 1, 2)).reshape(n, D_STATE, D_SSM)


def _state_from_kernel(st):
    n = st.shape[0]
    return jnp.transpose(st.reshape(n, D_STATE, SSM_HEADS, SSM_HEAD_DIM), (0, 2, 3, 1))


def kernel(x_prompt, x_sample, cache_conv_a, cache_ssm_conv, state_ssm, cache_ffn_conv, norm_mix, w_in, conv_a_w,
           conv_a_b, ln_a_g, ln_a_b, ssm_conv_w, ssm_conv_b, dt_bias, a_log, d_skip, ssm_norm, w_out, norm_ffn,
           w_up, ffn_conv_w, ffn_conv_b, w_down, norm_final):
    depth = w_in.shape[0]
    xp, xs = x_prompt, x_sample
    outs = [[] for _ in range(8)]
    for i in range(depth):
        p = _layer_params(norm_mix[i], w_in[i], conv_a_w[i], conv_a_b[i], ln_a_g[i], ln_a_b[i], ssm_conv_w[i],
                          ssm_conv_b[i], dt_bias[i], a_log[i], d_skip[i], ssm_norm[i], w_out[i], norm_ffn[i],
                          w_up[i], ffn_conv_w[i], ffn_conv_b[i], w_down[i])
        nfin = norm_final[None, :] if i == depth - 1 else None
        xp, pa, pb, ph = _mixer(xp, None, p)
        xp, pf = _ffn(xp, None, p, nfin)
        xs, sa, sb, sh = _mixer(xs, (cache_conv_a[i], cache_ssm_conv[i], _state_to_kernel(state_ssm[i])), p)
        xs, sf = _ffn(xs, cache_ffn_conv[i], p, nfin)
        for lst, v in zip(outs, (pa, pb, _state_from_kernel(ph), pf, sa, sb, _state_from_kernel(sh), sf)):
            lst.append(v)
    return (xp, xs) + tuple(jnp.stack(o) for o in outs)
```

```python
import functools
from typing import Any, NamedTuple

import jax
import jax.numpy as jnp
from jax import lax
from jax.experimental import pallas as pl
from jax.experimental.pallas import tpu as pltpu

D_MODEL = 1024
C_CONV = 1024
CONV_W = 31
SSM_HEADS = 16
SSM_HEAD_DIM = 64
D_SSM = SSM_HEADS * SSM_HEAD_DIM
SSM_GROUPS = 4
HEADS_PER_GROUP = SSM_HEADS // SSM_GROUPS
D_STATE = 128
SSM_CONV_W = 4
D_XBC = D_SSM + 2 * SSM_GROUPS * D_STATE
D_FF = 2816
FFN_CONV_W = 3
EPS = 1e-6
O_Z = 2 * C_CONV
O_XBC = O_Z + D_SSM
O_DT = O_XBC + D_XBC
LOG2_E = 1.4426950408889634

SUBLANES = 8
LANES = 128
MIXER_TILE = 512
FFN_TILE = 512
SSD_BLOCK = LANES
PACK_COLS = 512
CONV_ROWS = 128
PROJ_COLS = 256
PAD_A = 32
PAD_B = 8
PAD_F = 8
GROUP_W = HEADS_PER_GROUP * SSM_HEAD_DIM
VMEM_LIMIT_BYTES = 56 * 1024 * 1024

F32 = jnp.float32
BF16 = jnp.bfloat16


def _sigmoid(x):
    return jax.nn.sigmoid(x)


def _silu(x):
    return x * _sigmoid(x)


def _rmsnorm(x, g):
    return x * lax.rsqrt(jnp.mean(x * x, axis=-1, keepdims=True) + EPS) * g


def _softplus(x):
    return jnp.maximum(x, 0.0) + jnp.log1p(jnp.exp(-jnp.abs(x)))


def _split_bf16(v):
    hi = v.astype(BF16)
    lo = (v - hi.astype(F32)).astype(BF16)
    return hi, lo


def _dot(a, b):
    return jnp.dot(a, b, preferred_element_type=F32)


def _dot_split_lhs(v, m):
    hi, lo = _split_bf16(v)
    return _dot(hi, m) + _dot(lo, m)


def _dot_split_rhs(m, v):
    hi, lo = _split_bf16(v)
    return _dot(m, hi) + _dot(m, lo)


def _unpack(wp_ref, lo=0, hi=None, k_lo=0, k_hi=None):
    hi = wp_ref.shape[1] if hi is None else hi
    k_hi = 2 * wp_ref.shape[0] if k_hi is None else k_hi
    return pltpu.bitcast(wp_ref[k_lo // 2:k_hi // 2, lo:hi], BF16)


def _hist_init(ext_ref, pad, cache_ref):
    n_seg, n_slab = ext_ref.shape[0], ext_ref.shape[1]
    ext_ref[:, :, 0:pad, :] = jnp.zeros((n_seg, n_slab, pad, LANES), F32)
    if cache_ref is not None:
        ntail = cache_ref.shape[1]
        for c in range(n_slab):
            ext_ref[:, c, pad - ntail:pad, :] = cache_ref[:, :, c * LANES:(c + 1) * LANES]


def _hist_append(ext_ref, j, row, val, tail_ref, col0=0):
    s = val.shape[0]
    ntail = tail_ref.shape[1]
    for k in range(val.shape[1] // LANES):
        c = col0 // LANES + k
        ext_ref[j, c, row:row + s, :] = val[:, k * LANES:(k + 1) * LANES]
        tail_ref[j, :, c * LANES:(c + 1) * LANES] = ext_ref[j, c, row + s - ntail:row + s, :]


def _hist_carry(ext_ref, pad):
    end = ext_ref.shape[2]
    ext_ref[0, :, 0:pad, :] = ext_ref[0, :, end - pad:end, :]


def _dwconv(ext_ref, j, first_row, rows, w_ref, b_ref, emit, rb, slabs=None):
    ntaps = w_ref.shape[0]
    slabs = range(ext_ref.shape[1]) if slabs is None else slabs
    for r0 in range(0, rows, rb):
        for c in slabs:
            cols = pl.ds(c * LANES, LANES)
            acc = b_ref[:, cols]
            for k in range(ntaps):
                acc = acc + ext_ref[j, c, pl.ds(first_row + k + r0, rb), :] * w_ref[pl.ds(k, 1), cols]
            emit(r0, c * LANES, acc)


def _segment_masks(row0, seg_rows):
    ti = lax.broadcasted_iota(jnp.int32, (SSD_BLOCK, SSD_BLOCK), 0)
    si = lax.broadcasted_iota(jnp.int32, (SSD_BLOCK, SSD_BLOCK), 1)
    if seg_rows >= SSD_BLOCK:
        same = None
        causal = ti >= si
    else:
        shift = seg_rows.bit_length() - 1
        same = ((ti + row0) >> shift) == ((si + row0) >> shift)
        causal = same & (ti >= si)
    return same, causal


class _SsdBlock(NamedTuple):
    row0: int
    segs: Any
    causal: Any
    row_seg: Any
    acum2: Any
    stacked_t: Any
    ea_x: Any
    w_x: Any
    bd_x: Any


def _ssd_setup(c, seg_rows, dt_ref, alog_ref, e_ref):
    q = SSD_BLOCK
    assert seg_rows % q == 0 or q % seg_rows == 0
    row0 = c * q
    rows = pl.ds(row0, q)
    segs = range(row0 // seg_rows, (row0 + q - 1) // seg_rows + 1)
    multi = len(segs) > 1

    lane = lax.broadcasted_iota(jnp.int32, (1, LANES), 1)
    head_lane = lane < SSM_HEADS
    dt_lane = (lane >= SSM_HEADS) & (lane < 2 * SSM_HEADS)

    dt = dt_ref[rows, :]
    a_neg = jnp.where(head_lane, -jnp.exp(alog_ref[...]), 0.0)
    a = dt * a_neg

    same, causal = _segment_masks(row0, seg_rows)
    tri = jnp.where(causal, 1.0, 0.0).astype(BF16)
    acum = _dot_split_rhs(tri, a)
    if multi:
        seg_ones = jnp.where(same, 1.0, 0.0).astype(BF16)
        atot = _dot_split_rhs(seg_ones, a)
    else:
        atot = jnp.broadcast_to(acum[q - 1:q, :], (q, LANES))

    acum2 = acum * LOG2_E
    stacked_t = jnp.transpose(acum2 + jnp.where(dt_lane, dt, 0.0))

    e = _unpack(e_ref)
    ea_x = _dot_split_lhs(jnp.exp(acum), e)
    w_x = _dot_split_lhs(jnp.exp(atot - acum) * dt, e)
    seg_lo = [max(j * seg_rows, row0) - row0 for j in segs]
    first_rows = jnp.concatenate([atot[lo:lo + SUBLANES, :] for lo in seg_lo], axis=0)
    bd_x = _dot_split_lhs(jnp.exp(first_rows), e)

    row_seg = (lax.broadcasted_iota(jnp.int32, (q, 1), 0) + row0) // seg_rows if multi else None
    return _SsdBlock(row0, segs, causal, row_seg, acum2, stacked_t, ea_x, w_x, bd_x)


def _ssd_group(blk, g, seg_rows, xbcc_ref, dx_ref, state_ref, y_ref):
    q = SSD_BLOCK
    row0, segs, causal, row_seg, acum2, stacked_t, ea_x, w_x, bd_x = blk
    rows = pl.ds(row0, q)
    multi = len(segs) > 1
    head_cols = lax.broadcasted_iota(jnp.int32, (1, GROUP_W), 1) // SSM_HEAD_DIM
    gcols = pl.ds(g * GROUP_W, GROUP_W)
    xs_g = xbcc_ref[rows, gcols]
    b_g = xbcc_ref[rows, pl.ds(D_SSM + g * D_STATE, D_STATE)]
    c_g = xbcc_ref[rows, pl.ds(D_SSM + SSM_GROUPS * D_STATE + g * D_STATE, D_STATE)]
    b_bf = b_g.astype(BF16)
    c_bf = c_g.astype(BF16)
    xs_bf = xs_g.astype(BF16)

    cb = lax.dot_general(c_bf, b_bf, (((1,), (1,)), ((), ())), preferred_element_type=F32)
    m_parts = []
    x_parts = []
    for hh in range(HEADS_PER_GROUP):
        h = g * HEADS_PER_GROUP + hh
        seg = acum2[:, h:h + 1] - stacked_t[h:h + 1, :]
        decay = jnp.exp2(jnp.where(causal, seg, -jnp.inf))
        m = cb * decay * stacked_t[SSM_HEADS + h:SSM_HEADS + h + 1, :]
        m_parts.append(m.astype(BF16))
        x_parts.append(jnp.where(head_cols == hh, xs_bf, jnp.zeros_like(xs_bf)))
    y_g = _dot(jnp.concatenate(m_parts, axis=1), jnp.concatenate(x_parts, axis=0))

    b_t = jnp.transpose(b_g).astype(BF16)
    xdec = xs_g * w_x[:, g * GROUP_W:(g + 1) * GROUP_W]

    off_parts = []
    for i, j in enumerate(segs):
        lo = max(j * seg_rows, row0) - row0
        hi = min((j + 1) * seg_rows, row0 + q) - row0
        h_prev = state_ref[j, :, gcols]
        off_parts.append(_dot(c_bf[lo:hi, :], h_prev.astype(BF16)))
        if multi:
            xm = jnp.where(row_seg == j, xdec, 0.0).astype(BF16)
        else:
            xm = xdec.astype(BF16)
        bd = bd_x[SUBLANES * i:SUBLANES * i + 1, g * GROUP_W:(g + 1) * GROUP_W]
        state_ref[j, :, gcols] = h_prev * bd + _dot(b_t, xm)
    y_off = off_parts[0] if len(off_parts) == 1 else jnp.concatenate(off_parts, axis=0)
    y_g = y_g + y_off * ea_x[:, g * GROUP_W:(g + 1) * GROUP_W] + dx_ref[:, gcols] * xs_g
    y_ref[rows, gcols] = y_g


class _MixerW(NamedTuple):
    nm: Any
    win: Any
    wdt: Any
    dtb: Any
    alog: Any
    caw: Any
    cab: Any
    lng: Any
    lnb: Any
    scw: Any
    scb: Any
    dx: Any
    sn: Any
    wout: Any
    e: Any


def _project_and_conv(x, w, h_ref, uext_ref, xext_ref, z_ref, dt_ref, cva_ref, xbcc_ref, ta_ref, tb_ref,
                      seg_rows):
    s = seg_rows
    n_seg = x.shape[0] // s
    n_chunks = C_CONV // PROJ_COLS
    xbc_cols = D_XBC // n_chunks
    z_cols = D_SSM // n_chunks
    h_ref[...] = _rmsnorm(x, w.nm[...]).astype(BF16)
    for k in range(n_chunks):
        c0 = k * PROJ_COLS
        hb = h_ref[...]
        a = _dot(hb, _unpack(w.win, c0, c0 + PROJ_COLS))
        g = _dot(hb, _unpack(w.win, C_CONV + c0, C_CONV + c0 + PROJ_COLS))
        u = a * _sigmoid(g)
        for j in range(n_seg):
            _hist_append(uext_ref, j, PAD_A, u[j * s:(j + 1) * s, :], ta_ref, c0)

            def emit_a(r0, col, v, j=j):
                cva_ref[pl.ds(j * s + r0, v.shape[0]), pl.ds(col, v.shape[1])] = v

            _dwconv(uext_ref, j, PAD_A - (CONV_W - 1), s, w.caw, w.cab, emit_a, rb=min(s, CONV_ROWS),
                    slabs=range(c0 // LANES, (c0 + PROJ_COLS) // LANES))
        x0 = k * xbc_cols
        xbc = _dot(h_ref[...], _unpack(w.win, O_XBC + x0, O_XBC + x0 + xbc_cols))
        for j in range(n_seg):
            _hist_append(xext_ref, j, PAD_B, xbc[j * s:(j + 1) * s, :], tb_ref, x0)

            def emit_b(r0, col, v, j=j):
                xbcc_ref[pl.ds(j * s + r0, v.shape[0]), pl.ds(col, v.shape[1])] = _silu(v)

            _dwconv(xext_ref, j, PAD_B - (SSM_CONV_W - 1), s, w.scw, w.scb, emit_b, rb=min(s, CONV_ROWS),
                    slabs=range(x0 // LANES, (x0 + xbc_cols) // LANES))
        z0 = k * z_cols
        z_ref[:, z0:z0 + z_cols] = _dot(h_ref[...], _unpack(w.win, O_Z + z0, O_Z + z0 + z_cols))
    dt_ref[...] = _softplus(_dot(h_ref[...], _unpack(w.wdt)) + w.dtb[...])


def _mix(x_ref, out_ref, w, z_ref, dt_ref, state_ref, cva_ref, xbcc_ref, y_ref, mix_ref, seg_rows):
    s = seg_rows
    n_seg = x_ref.shape[0] // s

    uc = cva_ref[...]
    mu = jnp.mean(uc, axis=-1, keepdims=True)
    xc = uc - mu
    var = jnp.mean(xc * xc, axis=-1, keepdims=True)
    ln = xc * lax.rsqrt(var + EPS) * w.lng[...] + w.lnb[...]
    mix_ref[...] = _silu(ln).astype(BF16)
    out_ref[...] = x_ref[...] + _dot(mix_ref[...], _unpack(w.wout, k_hi=C_CONV))

    blocks = [_ssd_setup(c, s, dt_ref, w.alog, w.e) for c in range(n_seg * s // SSD_BLOCK)]
    for g in range(SSM_GROUPS):
        for blk in blocks:
            _ssd_group(blk, g, s, xbcc_ref, w.dx, state_ref, y_ref)
    for blk in blocks:
        rows = pl.ds(blk.row0, SSD_BLOCK)
        gated = y_ref[rows, :] * _silu(z_ref[rows, :])
        out_b = _rmsnorm(gated, w.sn[...]).astype(BF16)
        out_ref[rows, :] = out_ref[rows, :] + _dot(out_b, _unpack(w.wout, k_lo=C_CONV))


def _mixer_kernel(*refs, n_seg, seg_rows, has_cache, n_steps):
    it = iter(refs)
    x_ref = next(it)
    if has_cache:
        ca_ref, cb_ref, st_ref = next(it), next(it), next(it)
    w = _MixerW(*[next(it) for _ in range(len(_MixerW._fields))])
    x1_ref, ta_ref, tb_ref, state_ref = [next(it) for _ in range(4)]
    h_ref, uext_ref, xext_ref, cva_ref, xbcc_ref, z_ref, dt_ref, y_ref, mix_ref = [next(it) for _ in range(9)]

    s = seg_rows
    t = pl.program_id(1)

    @pl.when(t == 0)
    def _():
        _hist_init(uext_ref, PAD_A, ca_ref if has_cache else None)
        _hist_init(xext_ref, PAD_B, cb_ref if has_cache else None)
        if has_cache:
            state_ref[...] = st_ref[...]
        else:
            state_ref[...] = jnp.zeros(state_ref.shape, F32)

    _project_and_conv(x_ref[...], w, h_ref, uext_ref, xext_ref, z_ref, dt_ref, cva_ref, xbcc_ref, ta_ref, tb_ref, s)
    _mix(x_ref, x1_ref, w, z_ref, dt_ref, state_ref, cva_ref, xbcc_ref, y_ref, mix_ref, s)

    if n_steps > 1:
        _hist_carry(uext_ref, PAD_A)
        _hist_carry(xext_ref, PAD_B)


def _ffn_kernel(*refs, n_seg, seg_rows, has_cache, n_steps, final_norm):
    it = iter(refs)
    x1_ref = next(it)
    if has_cache:
        cf_ref = next(it)
    nf_ref, wup_ref, fcw_ref, fcb_ref, wd_ref = [next(it) for _ in range(5)]
    if final_norm:
        nfin_ref = next(it)
    y_ref, tf_ref = next(it), next(it)
    hf_ref, gext_ref, val_ref, act_ref = [next(it) for _ in range(4)]

    s = seg_rows
    t = pl.program_id(1)

    @pl.when(t == 0)
    def _():
        _hist_init(gext_ref, PAD_F, cf_ref if has_cache else None)

    x1 = x1_ref[...]
    hf_ref[...] = _rmsnorm(x1, nf_ref[...]).astype(BF16)
    hf = hf_ref[...]
    gate = _dot(hf, _unpack(wup_ref, 0, D_FF))
    val_ref[...] = _dot(hf, _unpack(wup_ref, D_FF, 2 * D_FF))
    for j in range(n_seg):
        _hist_append(gext_ref, j, PAD_F, gate[j * s:(j + 1) * s, :], tf_ref)

        def emit_f(r0, c0, v, j=j):
            rows = pl.ds(j * s + r0, v.shape[0])
            cols = pl.ds(c0, v.shape[1])
            act_ref[rows, cols] = (_silu(v) * val_ref[rows, cols]).astype(BF16)

        _dwconv(gext_ref, j, PAD_F - (FFN_CONV_W - 1), s, fcw_ref, fcb_ref, emit_f, rb=min(s, CONV_ROWS))
    x2 = x1 + _dot(act_ref[...], _unpack(wd_ref))
    if final_norm:
        x2 = _rmsnorm(x2, nfin_ref[...])
    y_ref[...] = x2

    if n_steps > 1:
        _hist_carry(gext_ref, PAD_F)


def _const_spec(arr):
    nd = arr.ndim
    return pl.BlockSpec(arr.shape, lambda b, t, _nd=nd: (0,) * _nd, pipeline_mode=pl.Buffered(1))


def _layout(n_streams, length, long_tile):
    if length >= long_tile:
        assert length % long_tile == 0
        return long_tile, 1, long_tile, (n_streams, length // long_tile)
    tile = SSD_BLOCK
    assert tile % length == 0 and (n_streams * length) % tile == 0 and length % SUBLANES == 0
    n_seg = tile // length
    return tile, n_seg, length, (n_streams // n_seg, 1)


def _compiler_params():
    return pltpu.CompilerParams(dimension_semantics=("arbitrary", "arbitrary"), vmem_limit_bytes=VMEM_LIMIT_BYTES)


def _mixer(x, caches, p):
    n_streams, length, _ = x.shape
    tile, n_seg, s, grid = _layout(n_streams, length, MIXER_TILE)
    has_cache = caches is not None
    xt = x.reshape(grid[0], grid[1] * tile, D_MODEL)

    tile_spec = pl.BlockSpec((None, tile, D_MODEL), lambda b, t: (b, t, 0))

    def per_batch(shape, **kw):
        return pl.BlockSpec((n_seg,) + shape, lambda b, t: (b, 0, 0), **kw)

    inputs = [xt]
    in_specs = [tile_spec]
    if has_cache:
        inputs += list(caches)
        once = dict(pipeline_mode=pl.Buffered(1))
        in_specs += [per_batch((CONV_W - 1, C_CONV), **once), per_batch((SSM_CONV_W - 1, D_XBC), **once),
                     per_batch((D_STATE, D_SSM), **once)]
    consts = [p["norm_mix"], p["w_in"], p["w_dt"], p["dt_bias"], p["a_log"],
              p["conv_a_w"], p["conv_a_b"], p["ln_a_g"], p["ln_a_b"], p["ssm_conv_w"], p["ssm_conv_b"],
              p["d_x"], p["ssm_norm"], p["w_out"], p["expand"]]
    inputs += consts
    in_specs += [_const_spec(c) for c in consts]

    out_shape = (
        jax.ShapeDtypeStruct(xt.shape, F32),
        jax.ShapeDtypeStruct((n_streams, CONV_W - 1, C_CONV), F32),
        jax.ShapeDtypeStruct((n_streams, SSM_CONV_W - 1, D_XBC), F32),
        jax.ShapeDtypeStruct((n_streams, D_STATE, D_SSM), F32),
    )
    out_specs = (tile_spec, per_batch((CONV_W - 1, C_CONV)), per_batch((SSM_CONV_W - 1, D_XBC)),
                 per_batch((D_STATE, D_SSM)))
    scratch = [
        pltpu.VMEM((tile, D_MODEL), BF16),
        pltpu.VMEM((n_seg, C_CONV // LANES, PAD_A + s, LANES), F32),
        pltpu.VMEM((n_seg, D_XBC // LANES, PAD_B + s, LANES), F32),
        pltpu.VMEM((tile, C_CONV), F32),
        pltpu.VMEM((tile, D_XBC), F32),
        pltpu.VMEM((tile, D_SSM), F32),
        pltpu.VMEM((tile, LANES), F32),
        pltpu.VMEM((tile, D_SSM), F32),
        pltpu.VMEM((tile, C_CONV), BF16),
    ]
    kern = functools.partial(_mixer_kernel, n_seg=n_seg, seg_rows=s, has_cache=has_cache, n_steps=grid[1])
    x1, ta, tb, st = pl.pallas_call(
        kern, grid=grid, in_specs=in_specs, out_specs=out_specs, out_shape=out_shape,
        scratch_shapes=scratch, name="mixer_cache" if has_cache else "mixer_fresh",
        compiler_params=_compiler_params(),
    )(*inputs)
    return x1.reshape(x.shape), ta, tb, st


def _ffn(x1, cache, p, norm_final):
    n_streams, length, _ = x1.shape
    tile, n_seg, s, grid = _layout(n_streams, length, FFN_TILE)
    has_cache = cache is not None
    xt = x1.reshape(grid[0], grid[1] * tile, D_MODEL)
    tile_spec = pl.BlockSpec((None, tile, D_MODEL), lambda b, t: (b, t, 0))
    tail_spec = pl.BlockSpec((n_seg, FFN_CONV_W - 1, D_FF), lambda b, t: (b, 0, 0))

    inputs = [xt]
    in_specs = [tile_spec]
    if has_cache:
        inputs.append(cache)
        in_specs.append(tail_spec)
    consts = [p["norm_ffn"], p["w_up"], p["ffn_conv_w"], p["ffn_conv_b"], p["w_down"]]
    if norm_final is not None:
        consts.append(norm_final)
    inputs += consts
    in_specs += [_const_spec(c) for c in consts]

    out_shape = (jax.ShapeDtypeStruct(xt.shape, F32),
                 jax.ShapeDtypeStruct((n_streams, FFN_CONV_W - 1, D_FF), F32))
    scratch = [
        pltpu.VMEM((tile, D_MODEL), BF16),
        pltpu.VMEM((n_seg, D_FF // LANES, PAD_F + s, LANES), F32),
        pltpu.VMEM((tile, D_FF), F32),
        pltpu.VMEM((tile, D_FF), BF16),
    ]
    kern = functools.partial(_ffn_kernel, n_seg=n_seg, seg_rows=s, has_cache=has_cache, n_steps=grid[1],
                             final_norm=norm_final is not None)
    y, tf = pl.pallas_call(
        kern, grid=grid, in_specs=in_specs, out_specs=(tile_spec, tail_spec), out_shape=out_shape,
        scratch_shapes=scratch, name="ffn_cache" if has_cache else "ffn_fresh",
        compiler_params=_compiler_params(),
    )(*inputs)
    return y.reshape(x1.shape), tf


def _pad_lanes(v, reps=1):
    v = jnp.tile(v.astype(F32), reps)
    return jnp.pad(v, (0, LANES - v.shape[0]))[None, :]


def _pack_kernel(w_ref, o_ref):
    o_ref[...] = pltpu.bitcast(w_ref[...].astype(BF16), jnp.uint32)


def _pack_bf16(w, n_cols=None):
    k, n = w.shape
    n_cols = n if n_cols is None else n_cols
    bn = min(PACK_COLS, n_cols)
    assert k % (2 * SUBLANES) == 0 and n_cols % bn == 0
    return pl.pallas_call(
        _pack_kernel, grid=(n_cols // bn,),
        in_specs=[pl.BlockSpec((k, bn), lambda c: (0, c))],
        out_specs=pl.BlockSpec((k // 2, bn), lambda c: (0, c)),
        out_shape=jax.ShapeDtypeStruct((k // 2, n_cols), jnp.uint32), name="pack_weights",
        compiler_params=pltpu.CompilerParams(dimension_semantics=("arbitrary",)),
    )(w)


def _layer_params(norm_mix, w_in, conv_a_w, conv_a_b, ln_a_g, ln_a_b, ssm_conv_w, ssm_conv_b, dt_bias, a_log,
                  d_skip, ssm_norm, w_out, norm_ffn, w_up, ffn_conv_w, ffn_conv_b, w_down):
    w_dt = w_in[:, O_DT:O_DT + SSM_HEADS]
    w_dt = jnp.pad(jnp.concatenate([w_dt, w_dt], axis=1), ((0, 0), (0, LANES - 2 * SSM_HEADS)))
    head_of_col = jnp.arange(D_SSM, dtype=jnp.int32) // SSM_HEAD_DIM
    expand = (jnp.arange(LANES, dtype=jnp.int32)[:, None] == head_of_col[None, :]).astype(F32)
    return {
        "norm_mix": norm_mix[None, :],
        "w_in": _pack_bf16(w_in, O_DT),
        "w_dt": _pack_bf16(w_dt),
        "dt_bias": _pad_lanes(dt_bias, 2),
        "a_log": _pad_lanes(a_log),
        "conv_a_w": conv_a_w, "conv_a_b": conv_a_b[None, :],
        "ln_a_g": ln_a_g[None, :], "ln_a_b": ln_a_b[None, :],
        "ssm_conv_w": ssm_conv_w, "ssm_conv_b": ssm_conv_b[None, :],
        "d_x": jnp.repeat(d_skip, SSM_HEAD_DIM)[None, :],
        "ssm_norm": ssm_norm[None, :],
        "w_out": _pack_bf16(w_out),
        "expand": _pack_bf16(expand),
        "norm_ffn": norm_ffn[None, :],
        "w_up": _pack_bf16(w_up),
        "ffn_conv_w": ffn_conv_w, "ffn_conv_b": ffn_conv_b[None, :],
        "w_down": _pack_bf16(w_down),
    }


def _state_to_kernel(st):
    n = st.shape[0]
    return jnp.transpose(st, (0, 3, 1, 2)).reshape(n, D_STATE, D_SSM)


def _state_from_kernel(st):
    n = st.shape[0]
    return jnp.transpose(st.reshape(n, D_STATE, SSM_HEADS, SSM_HEAD_DIM), (0, 2, 3, 1))


def kernel(x_prompt, x_sample, cache_conv_a, cache_ssm_conv, state_ssm, cache_ffn_conv, norm_mix, w_in, conv_a_w,
           conv_a_b, ln_a_g, ln_a_b, ssm_conv_w, ssm_conv_b, dt_bias, a_log, d_skip, ssm_norm, w_out, norm_ffn,
           w_up, ffn_conv_w, ffn_conv_b, w_down, norm_final):
    depth = w_in.shape[0]
    xp, xs = x_prompt, x_sample
    outs = [[] for _ in range(8)]
    for i in range(depth):
        p = _layer_params(norm_mix[i], w_in[i], conv_a_w[i], conv_a_b[i], ln_a_g[i], ln_a_b[i], ssm_conv_w[i],
                          ssm_conv_b[i], dt_bias[i], a_log[i], d_skip[i], ssm_norm[i], w_out[i], norm_ffn[i],
                          w_up[i], ffn_conv_w[i], ffn_conv_b[i], w_down[i])
        nfin = norm_final[None, :] if i == depth - 1 else None
        xp, pa, pb, ph = _mixer(xp, None, p)
        xp, pf = _ffn(xp, None, p, nfin)
        xs, sa, sb, sh = _mixer(xs, (cache_conv_a[i], cache_ssm_conv[i], _state_to_kernel(state_ssm[i])), p)
        xs, sf = _ffn(xs, cache_ffn_conv[i], p, nfin)
        for lst, v in zip(outs, (pa, pb, _state_from_kernel(ph), pf, sa, sb, _state_from_kernel(sh), sf)):
            lst.append(v)
    return (xp, xs) + tuple(jnp.stack(o) for o in outs)
```

```python
import functools
from typing import Any, NamedTuple

import jax
import jax.numpy as jnp
from jax import lax
from jax.experimental import pallas as pl
from jax.experimental.pallas import tpu as pltpu

D_MODEL = 1024
C_CONV = 1024
CONV_W = 31
SSM_HEADS = 16
SSM_HEAD_DIM = 64
D_SSM = SSM_HEADS * SSM_HEAD_DIM
SSM_GROUPS = 4
HEADS_PER_GROUP = SSM_HEADS // SSM_GROUPS
D_STATE = 128
SSM_CONV_W = 4
D_XBC = D_SSM + 2 * SSM_GROUPS * D_STATE
D_FF = 2816
FFN_CONV_W = 3
EPS = 1e-6
O_Z = 2 * C_CONV
O_XBC = O_Z + D_SSM
O_DT = O_XBC + D_XBC
LOG2_E = 1.4426950408889634

SUBLANES = 8
LANES = 128
MIXER_TILE = 512
FFN_TILE = 512
SSD_BLOCK = LANES
PACK_COLS = 512
CONV_ROWS = 32
PROJ_COLS = 256
PAD_A = 32
PAD_B = 8
PAD_F = 8
GROUP_W = HEADS_PER_GROUP * SSM_HEAD_DIM
VMEM_LIMIT_BYTES = 56 * 1024 * 1024

F32 = jnp.float32
BF16 = jnp.bfloat16


def _sigmoid(x):
    return jax.nn.sigmoid(x)


def _silu(x):
    return x * _sigmoid(x)


def _rmsnorm(x, g):
    return x * lax.rsqrt(jnp.mean(x * x, axis=-1, keepdims=True) + EPS) * g


def _softplus(x):
    return jnp.maximum(x, 0.0) + jnp.log1p(jnp.exp(-jnp.abs(x)))


def _split_bf16(v):
    hi = v.astype(BF16)
    lo = (v - hi.astype(F32)).astype(BF16)
    return hi, lo


def _dot(a, b):
    return jnp.dot(a, b, preferred_element_type=F32)


def _dot_split_lhs(v, m):
    hi, lo = _split_bf16(v)
    return _dot(hi, m) + _dot(lo, m)


def _dot_split_rhs(m, v):
    hi, lo = _split_bf16(v)
    return _dot(m, hi) + _dot(m, lo)


def _unpack(wp_ref, lo=0, hi=None, k_lo=0, k_hi=None):
    hi = wp_ref.shape[1] if hi is None else hi
    k_hi = 2 * wp_ref.shape[0] if k_hi is None else k_hi
    return pltpu.bitcast(wp_ref[k_lo // 2:k_hi // 2, lo:hi], BF16)


def _hist_init(ext_ref, pad, cache_ref):
    n_seg, n_slab = ext_ref.shape[0], ext_ref.shape[1]
    ext_ref[:, :, 0:pad, :] = jnp.zeros((n_seg, n_slab, pad, LANES), F32)
    if cache_ref is not None:
        ntail = cache_ref.shape[1]
        for c in range(n_slab):
            ext_ref[:, c, pad - ntail:pad, :] = cache_ref[:, :, c * LANES:(c + 1) * LANES]


def _hist_append(ext_ref, j, row, val, tail_ref, col0=0):
    s = val.shape[0]
    ntail = tail_ref.shape[1]
    for k in range(val.shape[1] // LANES):
        c = col0 // LANES + k
        ext_ref[j, c, row:row + s, :] = val[:, k * LANES:(k + 1) * LANES]
        tail_ref[j, :, c * LANES:(c + 1) * LANES] = ext_ref[j, c, row + s - ntail:row + s, :]


def _hist_carry(ext_ref, pad):
    end = ext_ref.shape[2]
    ext_ref[0, :, 0:pad, :] = ext_ref[0, :, end - pad:end, :]


def _dwconv(ext_ref, j, first_row, rows, w_ref, b_ref, emit, rb, slabs=None):
    ntaps = w_ref.shape[0]
    slabs = range(ext_ref.shape[1]) if slabs is None else slabs
    for r0 in range(0, rows, rb):
        for c in slabs:
            cols = pl.ds(c * LANES, LANES)
            acc = b_ref[:, cols]
            for k in range(ntaps):
                acc = acc + ext_ref[j, c, pl.ds(first_row + k + r0, rb), :] * w_ref[pl.ds(k, 1), cols]
            emit(r0, c * LANES, acc)


def _segment_masks(row0, seg_rows):
    ti = lax.broadcasted_iota(jnp.int32, (SSD_BLOCK, SSD_BLOCK), 0)
    si = lax.broadcasted_iota(jnp.int32, (SSD_BLOCK, SSD_BLOCK), 1)
    if seg_rows >= SSD_BLOCK:
        same = None
        causal = ti >= si
    else:
        shift = seg_rows.bit_length() - 1
        same = ((ti + row0) >> shift) == ((si + row0) >> shift)
        causal = same & (ti >= si)
    return same, causal


class _SsdBlock(NamedTuple):
    row0: int
    segs: Any
    causal: Any
    row_seg: Any
    acum2: Any
    stacked_t: Any
    ea_x: Any
    w_x: Any
    bd_x: Any


def _ssd_setup(c, seg_rows, dt_ref, alog_ref, e_ref):
    q = SSD_BLOCK
    assert seg_rows % q == 0 or q % seg_rows == 0
    row0 = c * q
    rows = pl.ds(row0, q)
    segs = range(row0 // seg_rows, (row0 + q - 1) // seg_rows + 1)
    multi = len(segs) > 1

    lane = lax.broadcasted_iota(jnp.int32, (1, LANES), 1)
    head_lane = lane < SSM_HEADS
    dt_lane = (lane >= SSM_HEADS) & (lane < 2 * SSM_HEADS)

    dt = dt_ref[rows, :]
    a_neg = jnp.where(head_lane, -jnp.exp(alog_ref[...]), 0.0)
    a = dt * a_neg

    same, causal = _segment_masks(row0, seg_rows)
    tri = jnp.where(causal, 1.0, 0.0).astype(BF16)
    acum = _dot_split_rhs(tri, a)
    if multi:
        seg_ones = jnp.where(same, 1.0, 0.0).astype(BF16)
        atot = _dot_split_rhs(seg_ones, a)
    else:
        atot = jnp.broadcast_to(acum[q - 1:q, :], (q, LANES))

    acum2 = acum * LOG2_E
    stacked_t = jnp.transpose(acum2 + jnp.where(dt_lane, dt, 0.0))

    e = _unpack(e_ref)
    ea_x = _dot_split_lhs(jnp.exp(acum), e)
    w_x = _dot_split_lhs(jnp.exp(atot - acum) * dt, e)
    seg_lo = [max(j * seg_rows, row0) - row0 for j in segs]
    first_rows = jnp.concatenate([atot[lo:lo + SUBLANES, :] for lo in seg_lo], axis=0)
    bd_x = _dot_split_lhs(jnp.exp(first_rows), e)

    row_seg = (lax.broadcasted_iota(jnp.int32, (q, 1), 0) + row0) // seg_rows if multi else None
    return _SsdBlock(row0, segs, causal, row_seg, acum2, stacked_t, ea_x, w_x, bd_x)


def _ssd_group(blk, g, seg_rows, xbcc_ref, dx_ref, state_ref, y_ref):
    q = SSD_BLOCK
    row0, segs, causal, row_seg, acum2, stacked_t, ea_x, w_x, bd_x = blk
    rows = pl.ds(row0, q)
    multi = len(segs) > 1
    head_cols = lax.broadcasted_iota(jnp.int32, (1, GROUP_W), 1) // SSM_HEAD_DIM
    gcols = pl.ds(g * GROUP_W, GROUP_W)
    xs_g = xbcc_ref[rows, gcols]
    b_g = xbcc_ref[rows, pl.ds(D_SSM + g * D_STATE, D_STATE)]
    c_g = xbcc_ref[rows, pl.ds(D_SSM + SSM_GROUPS * D_STATE + g * D_STATE, D_STATE)]
    b_bf = b_g.astype(BF16)
    c_bf = c_g.astype(BF16)
    xs_bf = xs_g.astype(BF16)

    cb = lax.dot_general(c_bf, b_bf, (((1,), (1,)), ((), ())), preferred_element_type=F32)
    m_parts = []
    x_parts = []
    for hh in range(HEADS_PER_GROUP):
        h = g * HEADS_PER_GROUP + hh
        seg = acum2[:, h:h + 1] - stacked_t[h:h + 1, :]
        decay = jnp.exp2(jnp.where(causal, seg, -jnp.inf))
        m = cb * decay * stacked_t[SSM_HEADS + h:SSM_HEADS + h + 1, :]
        m_parts.append(m.astype(BF16))
        x_parts.append(jnp.where(head_cols == hh, xs_bf, jnp.zeros_like(xs_bf)))
    y_g = _dot(jnp.concatenate(m_parts, axis=1), jnp.concatenate(x_parts, axis=0))

    b_t = jnp.transpose(b_g).astype(BF16)
    xdec = xs_g * w_x[:, g * GROUP_W:(g + 1) * GROUP_W]

    off_parts = []
    for i, j in enumerate(segs):
        lo = max(j * seg_rows, row0) - row0
        hi = min((j + 1) * seg_rows, row0 + q) - row0
        h_prev = state_ref[j, :, gcols]
        off_parts.append(_dot(c_bf[lo:hi, :], h_prev.astype(BF16)))
        if multi:
            xm = jnp.where(row_seg == j, xdec, 0.0).astype(BF16)
        else:
            xm = xdec.astype(BF16)
        bd = bd_x[SUBLANES * i:SUBLANES * i + 1, g * GROUP_W:(g + 1) * GROUP_W]
        state_ref[j, :, gcols] = h_prev * bd + _dot(b_t, xm)
    y_off = off_parts[0] if len(off_parts) == 1 else jnp.concatenate(off_parts, axis=0)
    y_g = y_g + y_off * ea_x[:, g * GROUP_W:(g + 1) * GROUP_W] + dx_ref[:, gcols] * xs_g
    y_ref[rows, gcols] = y_g


class _MixerW(NamedTuple):
    nm: Any
    win: Any
    wdt: Any
    dtb: Any
    alog: Any
    caw: Any
    cab: Any
    lng: Any
    lnb: Any
    scw: Any
    scb: Any
    dx: Any
    sn: Any
    wout: Any
    e: Any


def _project_and_conv(x, w, h_ref, uext_ref, xext_ref, z_ref, dt_ref, cva_ref, xbcc_ref, ta_ref, tb_ref,
                      seg_rows):
    s = seg_rows
    n_seg = x.shape[0] // s
    n_chunks = C_CONV // PROJ_COLS
    xbc_cols = D_XBC // n_chunks
    z_cols = D_SSM // n_chunks
    h_ref[...] = _rmsnorm(x, w.nm[...]).astype(BF16)
    for k in range(n_chunks):
        c0 = k * PROJ_COLS
        hb = h_ref[...]
        a = _dot(hb, _unpack(w.win, c0, c0 + PROJ_COLS))
        g = _dot(hb, _unpack(w.win, C_CONV + c0, C_CONV + c0 + PROJ_COLS))
        u = a * _sigmoid(g)
        for j in range(n_seg):
            _hist_append(uext_ref, j, PAD_A, u[j * s:(j + 1) * s, :], ta_ref, c0)

            def emit_a(r0, col, v, j=j):
                cva_ref[pl.ds(j * s + r0, v.shape[0]), pl.ds(col, v.shape[1])] = v

            _dwconv(uext_ref, j, PAD_A - (CONV_W - 1), s, w.caw, w.cab, emit_a, rb=min(s, CONV_ROWS),
                    slabs=range(c0 // LANES, (c0 + PROJ_COLS) // LANES))
        x0 = k * xbc_cols
        xbc = _dot(h_ref[...], _unpack(w.win, O_XBC + x0, O_XBC + x0 + xbc_cols))
        for j in range(n_seg):
            _hist_append(xext_ref, j, PAD_B, xbc[j * s:(j + 1) * s, :], tb_ref, x0)

            def emit_b(r0, col, v, j=j):
                xbcc_ref[pl.ds(j * s + r0, v.shape[0]), pl.ds(col, v.shape[1])] = _silu(v)

            _dwconv(xext_ref, j, PAD_B - (SSM_CONV_W - 1), s, w.scw, w.scb, emit_b, rb=min(s, CONV_ROWS),
                    slabs=range(x0 // LANES, (x0 + xbc_cols) // LANES))
        z0 = k * z_cols
        z_ref[:, z0:z0 + z_cols] = _dot(h_ref[...], _unpack(w.win, O_Z + z0, O_Z + z0 + z_cols))
    dt_ref[...] = _softplus(_dot(h_ref[...], _unpack(w.wdt)) + w.dtb[...])


def _mix(x_ref, out_ref, w, z_ref, dt_ref, state_ref, cva_ref, xbcc_ref, y_ref, mix_ref, seg_rows):
    s = seg_rows
    n_seg = x_ref.shape[0] // s

    uc = cva_ref[...]
    mu = jnp.mean(uc, axis=-1, keepdims=True)
    xc = uc - mu
    var = jnp.mean(xc * xc, axis=-1, keepdims=True)
    ln = xc * lax.rsqrt(var + EPS) * w.lng[...] + w.lnb[...]
    mix_ref[...] = _silu(ln).astype(BF16)
    out_ref[...] = x_ref[...] + _dot(mix_ref[...], _unpack(w.wout, k_hi=C_CONV))

    blocks = [_ssd_setup(c, s, dt_ref, w.alog, w.e) for c in range(n_seg * s // SSD_BLOCK)]
    for g in range(SSM_GROUPS):
        for blk in blocks:
            _ssd_group(blk, g, s, xbcc_ref, w.dx, state_ref, y_ref)
    for blk in blocks:
        rows = pl.ds(blk.row0, SSD_BLOCK)
        gated = y_ref[rows, :] * _silu(z_ref[rows, :])
        out_b = _rmsnorm(gated, w.sn[...]).astype(BF16)
        out_ref[rows, :] = out_ref[rows, :] + _dot(out_b, _unpack(w.wout, k_lo=C_CONV))


def _mixer_kernel(*refs, n_seg, seg_rows, has_cache, n_steps):
    it = iter(refs)
    x_ref = next(it)
    if has_cache:
        ca_ref, cb_ref, st_ref = next(it), next(it), next(it)
    w = _MixerW(*[next(it) for _ in range(len(_MixerW._fields))])
    x1_ref, ta_ref, tb_ref, state_ref = [next(it) for _ in range(4)]
    h_ref, uext_ref, xext_ref, cva_ref, xbcc_ref, z_ref, dt_ref, y_ref, mix_ref = [next(it) for _ in range(9)]

    s = seg_rows
    t = pl.program_id(1)

    @pl.when(t == 0)
    def _():
        _hist_init(uext_ref, PAD_A, ca_ref if has_cache else None)
        _hist_init(xext_ref, PAD_B, cb_ref if has_cache else None)
        if has_cache:
            state_ref[...] = st_ref[...]
        else:
            state_ref[...] = jnp.zeros(state_ref.shape, F32)

    _project_and_conv(x_ref[...], w, h_ref, uext_ref, xext_ref, z_ref, dt_ref, cva_ref, xbcc_ref, ta_ref, tb_ref, s)
    _mix(x_ref, x1_ref, w, z_ref, dt_ref, state_ref, cva_ref, xbcc_ref, y_ref, mix_ref, s)

    if n_steps > 1:
        _hist_carry(uext_ref, PAD_A)
        _hist_carry(xext_ref, PAD_B)


def _ffn_kernel(*refs, n_seg, seg_rows, has_cache, n_steps, final_norm):
    it = iter(refs)
    x1_ref = next(it)
    if has_cache:
        cf_ref = next(it)
    nf_ref, wup_ref, fcw_ref, fcb_ref, wd_ref = [next(it) for _ in range(5)]
    if final_norm:
        nfin_ref = next(it)
    y_ref, tf_ref = next(it), next(it)
    hf_ref, gext_ref, val_ref, act_ref = [next(it) for _ in range(4)]

    s = seg_rows
    t = pl.program_id(1)

    @pl.when(t == 0)
    def _():
        _hist_init(gext_ref, PAD_F, cf_ref if has_cache else None)

    x1 = x1_ref[...]
    hf_ref[...] = _rmsnorm(x1, nf_ref[...]).astype(BF16)
    hf = hf_ref[...]
    gate = _dot(hf, _unpack(wup_ref, 0, D_FF))
    val_ref[...] = _dot(hf, _unpack(wup_ref, D_FF, 2 * D_FF))
    for j in range(n_seg):
        _hist_append(gext_ref, j, PAD_F, gate[j * s:(j + 1) * s, :], tf_ref)

        def emit_f(r0, c0, v, j=j):
            rows = pl.ds(j * s + r0, v.shape[0])
            cols = pl.ds(c0, v.shape[1])
            act_ref[rows, cols] = (_silu(v) * val_ref[rows, cols]).astype(BF16)

        _dwconv(gext_ref, j, PAD_F - (FFN_CONV_W - 1), s, fcw_ref, fcb_ref, emit_f, rb=min(s, CONV_ROWS))
    x2 = x1 + _dot(act_ref[...], _unpack(wd_ref))
    if final_norm:
        x2 = _rmsnorm(x2, nfin_ref[...])
    y_ref[...] = x2

    if n_steps > 1:
        _hist_carry(gext_ref, PAD_F)


def _const_spec(arr):
    nd = arr.ndim
    return pl.BlockSpec(arr.shape, lambda b, t, _nd=nd: (0,) * _nd, pipeline_mode=pl.Buffered(1))


def _layout(n_streams, length, long_tile):
    if length >= long_tile:
        assert length % long_tile == 0
        return long_tile, 1, long_tile, (n_streams, length // long_tile)
    tile = SSD_BLOCK
    assert tile % length == 0 and (n_streams * length) % tile == 0 and length % SUBLANES == 0
    n_seg = tile // length
    return tile, n_seg, length, (n_streams // n_seg, 1)


def _compiler_params():
    return pltpu.CompilerParams(dimension_semantics=("arbitrary", "arbitrary"), vmem_limit_bytes=VMEM_LIMIT_BYTES)


def _mixer(x, caches, p):
    n_streams, length, _ = x.shape
    tile, n_seg, s, grid = _layout(n_streams, length, MIXER_TILE)
    has_cache = caches is not None
    xt = x.reshape(grid[0], grid[1] * tile, D_MODEL)

    tile_spec = pl.BlockSpec((None, tile, D_MODEL), lambda b, t: (b, t, 0))

    def per_batch(shape, **kw):
        return pl.BlockSpec((n_seg,) + shape, lambda b, t: (b, 0, 0), **kw)

    inputs = [xt]
    in_specs = [tile_spec]
    if has_cache:
        inputs += list(caches)
        once = dict(pipeline_mode=pl.Buffered(1))
        in_specs += [per_batch((CONV_W - 1, C_CONV), **once), per_batch((SSM_CONV_W - 1, D_XBC), **once),
                     per_batch((D_STATE, D_SSM), **once)]
    consts = [p["norm_mix"], p["w_in"], p["w_dt"], p["dt_bias"], p["a_log"],
              p["conv_a_w"], p["conv_a_b"], p["ln_a_g"], p["ln_a_b"], p["ssm_conv_w"], p["ssm_conv_b"],
              p["d_x"], p["ssm_norm"], p["w_out"], p["expand"]]
    inputs += consts
    in_specs += [_const_spec(c) for c in consts]

    out_shape = (
        jax.ShapeDtypeStruct(xt.shape, F32),
        jax.ShapeDtypeStruct((n_streams, CONV_W - 1, C_CONV), F32),
        jax.ShapeDtypeStruct((n_streams, SSM_CONV_W - 1, D_XBC), F32),
        jax.ShapeDtypeStruct((n_streams, D_STATE, D_SSM), F32),
    )
    out_specs = (tile_spec, per_batch((CONV_W - 1, C_CONV)), per_batch((SSM_CONV_W - 1, D_XBC)),
                 per_batch((D_STATE, D_SSM)))
    scratch = [
        pltpu.VMEM((tile, D_MODEL), BF16),
        pltpu.VMEM((n_seg, C_CONV // LANES, PAD_A + s, LANES), F32),
        pltpu.VMEM((n_seg, D_XBC // LANES, PAD_B + s, LANES), F32),
        pltpu.VMEM((tile, C_CONV), F32),
        pltpu.VMEM((tile, D_XBC), F32),
        pltpu.VMEM((tile, D_SSM), F32),
        pltpu.VMEM((tile, LANES), F32),
        pltpu.VMEM((tile, D_SSM), F32),
        pltpu.VMEM((tile, C_CONV), BF16),
    ]
    kern = functools.partial(_mixer_kernel, n_seg=n_seg, seg_rows=s, has_cache=has_cache, n_steps=grid[1])
    x1, ta, tb, st = pl.pallas_call(
        kern, grid=grid, in_specs=in_specs, out_specs=out_specs, out_shape=out_shape,
        scratch_shapes=scratch, name="mixer_cache" if has_cache else "mixer_fresh",
        compiler_params=_compiler_params(),
    )(*inputs)
    return x1.reshape(x.shape), ta, tb, st


def _ffn(x1, cache, p, norm_final):
    n_streams, length, _ = x1.shape
    tile, n_seg, s, grid = _layout(n_streams, length, FFN_TILE)
    has_cache = cache is not None
    xt = x1.reshape(grid[0], grid[1] * tile, D_MODEL)
    tile_spec = pl.BlockSpec((None, tile, D_MODEL), lambda b, t: (b, t, 0))
    tail_spec = pl.BlockSpec((n_seg, FFN_CONV_W - 1, D_FF), lambda b, t: (b, 0, 0))

    inputs = [xt]
    in_specs = [tile_spec]
    if has_cache:
        inputs.append(cache)
        in_specs.append(tail_spec)
    consts = [p["norm_ffn"], p["w_up"], p["ffn_conv_w"], p["ffn_conv_b"], p["w_down"]]
    if norm_final is not None:
        consts.append(norm_final)
    inputs += consts
    in_specs += [_const_spec(c) for c in consts]

    out_shape = (jax.ShapeDtypeStruct(xt.shape, F32),
                 jax.ShapeDtypeStruct((n_streams, FFN_CONV_W - 1, D_FF), F32))
    scratch = [
        pltpu.VMEM((tile, D_MODEL), BF16),
        pltpu.VMEM((n_seg, D_FF // LANES, PAD_F + s, LANES), F32),
        pltpu.VMEM((tile, D_FF), F32),
        pltpu.VMEM((tile, D_FF), BF16),
    ]
    kern = functools.partial(_ffn_kernel, n_seg=n_seg, seg_rows=s, has_cache=has_cache, n_steps=grid[1],
                             final_norm=norm_final is not None)
    y, tf = pl.pallas_call(
        kern, grid=grid, in_specs=in_specs, out_specs=(tile_spec, tail_spec), out_shape=out_shape,
        scratch_shapes=scratch, name="ffn_cache" if has_cache else "ffn_fresh",
        compiler_params=_compiler_params(),
    )(*inputs)
    return y.reshape(x1.shape), tf


def _pad_lanes(v, reps=1):
    v = jnp.tile(v.astype(F32), reps)
    return jnp.pad(v, (0, LANES - v.shape[0]))[None, :]


def _pack_kernel(w_ref, o_ref):
    o_ref[...] = pltpu.bitcast(w_ref[...].astype(BF16), jnp.uint32)


def _pack_bf16(w, n_cols=None):
    k, n = w.shape
    n_cols = n if n_cols is None else n_cols
    bn = min(PACK_COLS, n_cols)
    assert k % (2 * SUBLANES) == 0 and n_cols % bn == 0
    return pl.pallas_call(
        _pack_kernel, grid=(n_cols // bn,),
        in_specs=[pl.BlockSpec((k, bn), lambda c: (0, c))],
        out_specs=pl.BlockSpec((k // 2, bn), lambda c: (0, c)),
        out_shape=jax.ShapeDtypeStruct((k // 2, n_cols), jnp.uint32), name="pack_weights",
        compiler_params=pltpu.CompilerParams(dimension_semantics=("arbitrary",)),
    )(w)


def _layer_params(norm_mix, w_in, conv_a_w, conv_a_b, ln_a_g, ln_a_b, ssm_conv_w, ssm_conv_b, dt_bias, a_log,
                  d_skip, ssm_norm, w_out, norm_ffn, w_up, ffn_conv_w, ffn_conv_b, w_down):
    w_dt = w_in[:, O_DT:O_DT + SSM_HEADS]
    w_dt = jnp.pad(jnp.concatenate([w_dt, w_dt], axis=1), ((0, 0), (0, LANES - 2 * SSM_HEADS)))
    head_of_col = jnp.arange(D_SSM, dtype=jnp.int32) // SSM_HEAD_DIM
    expand = (jnp.arange(LANES, dtype=jnp.int32)[:, None] == head_of_col[None, :]).astype(F32)
    return {
        "norm_mix": norm_mix[None, :],
        "w_in": _pack_bf16(w_in, O_DT),
        "w_dt": _pack_bf16(w_dt),
        "dt_bias": _pad_lanes(dt_bias, 2),
        "a_log": _pad_lanes(a_log),
        "conv_a_w": conv_a_w, "conv_a_b": conv_a_b[None, :],
        "ln_a_g": ln_a_g[None, :], "ln_a_b": ln_a_b[None, :],
        "ssm_conv_w": ssm_conv_w, "ssm_conv_b": ssm_conv_b[None, :],
        "d_x": jnp.repeat(d_skip, SSM_HEAD_DIM)[None, :],
        "ssm_norm": ssm_norm[None, :],
        "w_out": _pack_bf16(w_out),
        "expand": _pack_bf16(expand),
        "norm_ffn": norm_ffn[None, :],
        "w_up": _pack_bf16(w_up),
        "ffn_conv_w": ffn_conv_w, "ffn_conv_b": ffn_conv_b[None, :],
        "w_down": _pack_bf16(w_down),
    }


def _state_to_kernel(st):
    n = st.shape[0]
    return jnp.transpose(st, (0, 3, 1, 2)).reshape(n, D_STATE, D_SSM)


def _state_from_kernel(st):
    n = st.shape[0]
    return jnp.transpose(st.reshape(n, D_STATE, SSM_HEADS, SSM_HEAD_DIM), (0, 2, 3, 1))


def kernel(x_prompt, x_sample, cache_conv_a, cache_ssm_conv, state_ssm, cache_ffn_conv, norm_mix, w_in, conv_a_w,
           conv_a_b, ln_a_g, ln_a_b, ssm_conv_w, ssm_conv_b, dt_bias, a_log, d_skip, ssm_norm, w_out, norm_ffn,
           w_up, ffn_conv_w, ffn_conv_b, w_down, norm_final):
    depth = w_in.shape[0]
    xp, xs = x_prompt, x_sample
    outs = [[] for _ in range(8)]
    for i in range(depth):
        p = _layer_params(norm_mix[i], w_in[i], conv_a_w[i], conv_a_b[i], ln_a_g[i], ln_a_b[i], ssm_conv_w[i],
                          ssm_conv_b[i], dt_bias[i], a_log[i], d_skip[i], ssm_norm[i], w_out[i], norm_ffn[i],
                          w_up[i], ffn_conv_w[i], ffn_conv_b[i], w_down[i])
        nfin = norm_final[None, :] if i == depth - 1 else None
        xp, pa, pb, ph = _mixer(xp, None, p)
        xp, pf = _ffn(xp, None, p, nfin)
        xs, sa, sb, sh = _mixer(xs, (cache_conv_a[i], cache_ssm_conv[i], _state_to_kernel(state_ssm[i])), p)
        xs, sf = _ffn(xs, cache_ffn_conv[i], p, nfin)
        for lst, v in zip(outs, (pa, pb, _state_from_kernel(ph), pf, sa, sb, _state_from_kernel(sh), sf)):
            lst.append(v)
    return (xp, xs) + tuple(jnp.stack(o) for o in outs)
```

```python
import functools
from typing import Any, NamedTuple

import jax
import jax.numpy as jnp
from jax import lax
from jax.experimental import pallas as pl
from jax.experimental.pallas import tpu as pltpu

D_MODEL = 1024
C_CONV = 1024
CONV_W = 31
SSM_HEADS = 16
SSM_HEAD_DIM = 64
D_SSM = SSM_HEADS * SSM_HEAD_DIM
SSM_GROUPS = 4
HEADS_PER_GROUP = SSM_HEADS // SSM_GROUPS
D_STATE = 128
SSM_CONV_W = 4
D_XBC = D_SSM + 2 * SSM_GROUPS * D_STATE
D_FF = 2816
FFN_CONV_W = 3
EPS = 1e-6
O_Z = 2 * C_CONV
O_XBC = O_Z + D_SSM
O_DT = O_XBC + D_XBC
LOG2_E = 1.4426950408889634

SUBLANES = 8
LANES = 128
MIXER_TILE = 512
FFN_TILE = 512
SSD_BLOCK = LANES
PACK_COLS = 512
CONV_ROWS = 32
PROJ_COLS = 256
PAD_A = 32
PAD_B = 8
PAD_F = 8
GROUP_W = HEADS_PER_GROUP * SSM_HEAD_DIM
VMEM_LIMIT_BYTES = 56 * 1024 * 1024

F32 = jnp.float32
BF16 = jnp.bfloat16


def _sigmoid(x):
    return jax.nn.sigmoid(x)


def _silu(x):
    return x * _sigmoid(x)


def _rmsnorm(x, g):
    return x * lax.rsqrt(jnp.mean(x * x, axis=-1, keepdims=True) + EPS) * g


def _softplus(x):
    return jnp.maximum(x, 0.0) + jnp.log1p(jnp.exp(-jnp.abs(x)))


def _split_bf16(v):
    hi = v.astype(BF16)
    lo = (v - hi.astype(F32)).astype(BF16)
    return hi, lo


def _dot(a, b):
    return jnp.dot(a, b, preferred_element_type=F32)


def _dot_split_lhs(v, m):
    hi, lo = _split_bf16(v)
    return _dot(hi, m) + _dot(lo, m)


def _dot_split_rhs(m, v):
    hi, lo = _split_bf16(v)
    return _dot(m, hi) + _dot(m, lo)


def _unpack(wp_ref, lo=0, hi=None, k_lo=0, k_hi=None):
    hi = wp_ref.shape[1] if hi is None else hi
    k_hi = 2 * wp_ref.shape[0] if k_hi is None else k_hi
    return pltpu.bitcast(wp_ref[k_lo // 2:k_hi // 2, lo:hi], BF16)


def _hist_init(ext_ref, pad, cache_ref):
    n_seg, n_slab = ext_ref.shape[0], ext_ref.shape[1]
    ext_ref[:, :, 0:pad, :] = jnp.zeros((n_seg, n_slab, pad, LANES), F32)
    if cache_ref is not None:
        ntail = cache_ref.shape[1]
        for c in range(n_slab):
            ext_ref[:, c, pad - ntail:pad, :] = cache_ref[:, :, c * LANES:(c + 1) * LANES]


def _hist_append(ext_ref, j, row, val, tail_ref, col0=0):
    s = val.shape[0]
    ntail = tail_ref.shape[1]
    for k in range(val.shape[1] // LANES):
        c = col0 // LANES + k
        ext_ref[j, c, row:row + s, :] = val[:, k * LANES:(k + 1) * LANES]
        tail_ref[j, :, c * LANES:(c + 1) * LANES] = ext_ref[j, c, row + s - ntail:row + s, :]


def _hist_carry(ext_ref, pad):
    end = ext_ref.shape[2]
    ext_ref[0, :, 0:pad, :] = ext_ref[0, :, end - pad:end, :]


def _dwconv(ext_ref, j, first_row, rows, w_ref, b_ref, emit, rb, slabs=None):
    ntaps = w_ref.shape[0]
    slabs = range(ext_ref.shape[1]) if slabs is None else slabs
    for r0 in range(0, rows, rb):
        for c in slabs:
            cols = pl.ds(c * LANES, LANES)
            acc = b_ref[:, cols]
            for k in range(ntaps):
                acc = acc + ext_ref[j, c, pl.ds(first_row + k + r0, rb), :] * w_ref[pl.ds(k, 1), cols]
            emit(r0, c * LANES, acc)


def _segment_masks(row0, seg_rows):
    ti = lax.broadcasted_iota(jnp.int32, (SSD_BLOCK, SSD_BLOCK), 0)
    si = lax.broadcasted_iota(jnp.int32, (SSD_BLOCK, SSD_BLOCK), 1)
    if seg_rows >= SSD_BLOCK:
        same = None
        causal = ti >= si
    else:
        shift = seg_rows.bit_length() - 1
        same = ((ti + row0) >> shift) == ((si + row0) >> shift)
        causal = same & (ti >= si)
    return same, causal


class _SsdBlock(NamedTuple):
    row0: int
    segs: Any
    causal: Any
    row_seg: Any
    acum2: Any
    stacked_t: Any
    ea_x: Any
    w_x: Any
    bd_x: Any


def _ssd_setup(c, seg_rows, dt_ref, alog_ref, e_ref):
    q = SSD_BLOCK
    assert seg_rows % q == 0 or q % seg_rows == 0
    row0 = c * q
    rows = pl.ds(row0, q)
    segs = range(row0 // seg_rows, (row0 + q - 1) // seg_rows + 1)
    multi = len(segs) > 1

    lane = lax.broadcasted_iota(jnp.int32, (1, LANES), 1)
    head_lane = lane < SSM_HEADS
    dt_lane = (lane >= SSM_HEADS) & (lane < 2 * SSM_HEADS)

    dt = dt_ref[rows, :]
    a_neg = jnp.where(head_lane, -jnp.exp(alog_ref[...]), 0.0)
    a = dt * a_neg

    same, causal = _segment_masks(row0, seg_rows)
    tri = jnp.where(causal, 1.0, 0.0).astype(BF16)
    acum = _dot_split_rhs(tri, a)
    if multi:
        seg_ones = jnp.where(same, 1.0, 0.0).astype(BF16)
        atot = _dot_split_rhs(seg_ones, a)
    else:
        atot = jnp.broadcast_to(acum[q - 1:q, :], (q, LANES))

    acum2 = acum * LOG2_E
    stacked_t = jnp.transpose(acum2 + jnp.where(dt_lane, dt, 0.0))

    e = _unpack(e_ref)
    ea_x = _dot_split_lhs(jnp.exp(acum), e)
    w_x = _dot_split_lhs(jnp.exp(atot - acum) * dt, e)
    seg_lo = [max(j * seg_rows, row0) - row0 for j in segs]
    first_rows = jnp.concatenate([atot[lo:lo + SUBLANES, :] for lo in seg_lo], axis=0)
    bd_x = _dot_split_lhs(jnp.exp(first_rows), e)

    row_seg = (lax.broadcasted_iota(jnp.int32, (q, 1), 0) + row0) // seg_rows if multi else None
    return _SsdBlock(row0, segs, causal, row_seg, acum2, stacked_t, ea_x, w_x, bd_x)


def _ssd_group(blk, g, seg_rows, xbcc_ref, dx_ref, state_ref, y_ref):
    q = SSD_BLOCK
    row0, segs, causal, row_seg, acum2, stacked_t, ea_x, w_x, bd_x = blk
    rows = pl.ds(row0, q)
    multi = len(segs) > 1
    head_cols = lax.broadcasted_iota(jnp.int32, (1, GROUP_W), 1) // SSM_HEAD_DIM
    gcols = pl.ds(g * GROUP_W, GROUP_W)
    xs_g = xbcc_ref[rows, gcols]
    b_g = xbcc_ref[rows, pl.ds(D_SSM + g * D_STATE, D_STATE)]
    c_g = xbcc_ref[rows, pl.ds(D_SSM + SSM_GROUPS * D_STATE + g * D_STATE, D_STATE)]
    b_bf = b_g.astype(BF16)
    c_bf = c_g.astype(BF16)
    xs_bf = xs_g.astype(BF16)

    cb = lax.dot_general(c_bf, b_bf, (((1,), (1,)), ((), ())), preferred_element_type=F32)
    m_parts = []
    x_parts = []
    for hh in range(HEADS_PER_GROUP):
        h = g * HEADS_PER_GROUP + hh
        seg = acum2[:, h:h + 1] - stacked_t[h:h + 1, :]
        decay = jnp.exp2(jnp.where(causal, seg, -jnp.inf))
        m = cb * decay * stacked_t[SSM_HEADS + h:SSM_HEADS + h + 1, :]
        m_parts.append(m.astype(BF16))
        x_parts.append(jnp.where(head_cols == hh, xs_bf, jnp.zeros_like(xs_bf)))
    y_g = _dot(jnp.concatenate(m_parts, axis=1), jnp.concatenate(x_parts, axis=0))

    b_t = jnp.transpose(b_g).astype(BF16)
    xdec = xs_g * w_x[:, g * GROUP_W:(g + 1) * GROUP_W]

    off_parts = []
    for i, j in enumerate(segs):
        lo = max(j * seg_rows, row0) - row0
        hi = min((j + 1) * seg_rows, row0 + q) - row0
        h_prev = state_ref[j, :, gcols]
        off_parts.append(_dot(c_bf[lo:hi, :], h_prev.astype(BF16)))
        if multi:
            xm = jnp.where(row_seg == j, xdec, 0.0).astype(BF16)
        else:
            xm = xdec.astype(BF16)
        bd = bd_x[SUBLANES * i:SUBLANES * i + 1, g * GROUP_W:(g + 1) * GROUP_W]
        state_ref[j, :, gcols] = h_prev * bd + _dot(b_t, xm)
    y_off = off_parts[0] if len(off_parts) == 1 else jnp.concatenate(off_parts, axis=0)
    y_g = y_g + y_off * ea_x[:, g * GROUP_W:(g + 1) * GROUP_W] + dx_ref[:, gcols] * xs_g
    y_ref[rows, gcols] = y_g


class _MixerW(NamedTuple):
    nm: Any
    win: Any
    wdt: Any
    dtb: Any
    alog: Any
    caw: Any
    cab: Any
    lng: Any
    lnb: Any
    scw: Any
    scb: Any
    dx: Any
    sn: Any
    wout: Any
    e: Any


def _project_and_conv(x, w, h_ref, uext_ref, xext_ref, z_ref, dt_ref, cva_ref, xbcc_ref, ta_ref, tb_ref,
                      seg_rows):
    s = seg_rows
    n_seg = x.shape[0] // s
    n_chunks = C_CONV // PROJ_COLS
    xbc_cols = D_XBC // n_chunks
    z_cols = D_SSM // n_chunks
    h_ref[...] = _rmsnorm(x, w.nm[...]).astype(BF16)
    for k in range(n_chunks):
        c0 = k * PROJ_COLS
        hb = h_ref[...]
        a = _dot(hb, _unpack(w.win, c0, c0 + PROJ_COLS))
        g = _dot(hb, _unpack(w.win, C_CONV + c0, C_CONV + c0 + PROJ_COLS))
        u = a * _sigmoid(g)
        for j in range(n_seg):
            _hist_append(uext_ref, j, PAD_A, u[j * s:(j + 1) * s, :], ta_ref, c0)

            def emit_a(r0, col, v, j=j):
                cva_ref[pl.ds(j * s + r0, v.shape[0]), pl.ds(col, v.shape[1])] = v

            _dwconv(uext_ref, j, PAD_A - (CONV_W - 1), s, w.caw, w.cab, emit_a, rb=min(s, CONV_ROWS),
                    slabs=range(c0 // LANES, (c0 + PROJ_COLS) // LANES))
        x0 = k * xbc_cols
        xbc = _dot(h_ref[...], _unpack(w.win, O_XBC + x0, O_XBC + x0 + xbc_cols))
        for j in range(n_seg):
            _hist_append(xext_ref, j, PAD_B, xbc[j * s:(j + 1) * s, :], tb_ref, x0)

            def emit_b(r0, col, v, j=j):
                xbcc_ref[pl.ds(j * s + r0, v.shape[0]), pl.ds(col, v.shape[1])] = _silu(v)

            _dwconv(xext_ref, j, PAD_B - (SSM_CONV_W - 1), s, w.scw, w.scb, emit_b, rb=min(s, CONV_ROWS),
                    slabs=range(x0 // LANES, (x0 + xbc_cols) // LANES))
        z0 = k * z_cols
        z_ref[:, z0:z0 + z_cols] = _dot(h_ref[...], _unpack(w.win, O_Z + z0, O_Z + z0 + z_cols))
    dt_ref[...] = _softplus(_dot(h_ref[...], _unpack(w.wdt)) + w.dtb[...])


def _mix(x_ref, out_ref, w, z_ref, dt_ref, state_ref, cva_ref, xbcc_ref, y_ref, mix_ref, seg_rows):
    s = seg_rows
    n_seg = x_ref.shape[0] // s

    uc = cva_ref[...]
    mu = jnp.mean(uc, axis=-1, keepdims=True)
    xc = uc - mu
    var = jnp.mean(xc * xc, axis=-1, keepdims=True)
    ln = xc * lax.rsqrt(var + EPS) * w.lng[...] + w.lnb[...]
    mix_ref[...] = _silu(ln).astype(BF16)
    out_ref[...] = x_ref[...] + _dot(mix_ref[...], _unpack(w.wout, k_hi=C_CONV))

    blocks = [_ssd_setup(c, s, dt_ref, w.alog, w.e) for c in range(n_seg * s // SSD_BLOCK)]
    for blk in blocks:
        for g in range(SSM_GROUPS):
            _ssd_group(blk, g, s, xbcc_ref, w.dx, state_ref, y_ref)
    for blk in blocks:
        rows = pl.ds(blk.row0, SSD_BLOCK)
        gated = y_ref[rows, :] * _silu(z_ref[rows, :])
        out_b = _rmsnorm(gated, w.sn[...]).astype(BF16)
        out_ref[rows, :] = out_ref[rows, :] + _dot(out_b, _unpack(w.wout, k_lo=C_CONV))


def _mixer_kernel(*refs, n_seg, seg_rows, has_cache, n_steps):
    it = iter(refs)
    x_ref = next(it)
    if has_cache:
        ca_ref, cb_ref, st_ref = next(it), next(it), next(it)
    w = _MixerW(*[next(it) for _ in range(len(_MixerW._fields))])
    x1_ref, ta_ref, tb_ref, state_ref = [next(it) for _ in range(4)]
    h_ref, uext_ref, xext_ref, cva_ref, xbcc_ref, z_ref, dt_ref, y_ref, mix_ref = [next(it) for _ in range(9)]

    s = seg_rows
    t = pl.program_id(1)

    @pl.when(t == 0)
    def _():
        _hist_init(uext_ref, PAD_A, ca_ref if has_cache else None)
        _hist_init(xext_ref, PAD_B, cb_ref if has_cache else None)
        if has_cache:
            state_ref[...] = st_ref[...]
        else:
            state_ref[...] = jnp.zeros(state_ref.shape, F32)

    _project_and_conv(x_ref[...], w, h_ref, uext_ref, xext_ref, z_ref, dt_ref, cva_ref, xbcc_ref, ta_ref, tb_ref, s)
    _mix(x_ref, x1_ref, w, z_ref, dt_ref, state_ref, cva_ref, xbcc_ref, y_ref, mix_ref, s)

    if n_steps > 1:
        _hist_carry(uext_ref, PAD_A)
        _hist_carry(xext_ref, PAD_B)


def _ffn_kernel(*refs, n_seg, seg_rows, has_cache, n_steps, final_norm):
    it = iter(refs)
    x1_ref = next(it)
    if has_cache:
        cf_ref = next(it)
    nf_ref, wup_ref, fcw_ref, fcb_ref, wd_ref = [next(it) for _ in range(5)]
    if final_norm:
        nfin_ref = next(it)
    y_ref, tf_ref = next(it), next(it)
    hf_ref, gext_ref, val_ref, act_ref = [next(it) for _ in range(4)]

    s = seg_rows
    t = pl.program_id(1)

    @pl.when(t == 0)
    def _():
        _hist_init(gext_ref, PAD_F, cf_ref if has_cache else None)

    x1 = x1_ref[...]
    hf_ref[...] = _rmsnorm(x1, nf_ref[...]).astype(BF16)
    hf = hf_ref[...]
    gate = _dot(hf, _unpack(wup_ref, 0, D_FF))
    val_ref[...] = _dot(hf, _unpack(wup_ref, D_FF, 2 * D_FF))
    for j in range(n_seg):
        _hist_append(gext_ref, j, PAD_F, gate[j * s:(j + 1) * s, :], tf_ref)

        def emit_f(r0, c0, v, j=j):
            rows = pl.ds(j * s + r0, v.shape[0])
            cols = pl.ds(c0, v.shape[1])
            act_ref[rows, cols] = (_silu(v) * val_ref[rows, cols]).astype(BF16)

        _dwconv(gext_ref, j, PAD_F - (FFN_CONV_W - 1), s, fcw_ref, fcb_ref, emit_f, rb=min(s, CONV_ROWS))
    x2 = x1 + _dot(act_ref[...], _unpack(wd_ref))
    if final_norm:
        x2 = _rmsnorm(x2, nfin_ref[...])
    y_ref[...] = x2

    if n_steps > 1:
        _hist_carry(gext_ref, PAD_F)


def _const_spec(arr):
    nd = arr.ndim
    return pl.BlockSpec(arr.shape, lambda b, t, _nd=nd: (0,) * _nd, pipeline_mode=pl.Buffered(1))


def _layout(n_streams, length, long_tile):
    if length >= long_tile:
        assert length % long_tile == 0
        return long_tile, 1, long_tile, (n_streams, length // long_tile)
    tile = SSD_BLOCK
    assert tile % length == 0 and (n_streams * length) % tile == 0 and length % SUBLANES == 0
    n_seg = tile // length
    return tile, n_seg, length, (n_streams // n_seg, 1)


def _compiler_params():
    return pltpu.CompilerParams(dimension_semantics=("arbitrary", "arbitrary"), vmem_limit_bytes=VMEM_LIMIT_BYTES)


def _mixer(x, caches, p):
    n_streams, length, _ = x.shape
    tile, n_seg, s, grid = _layout(n_streams, length, MIXER_TILE)
    has_cache = caches is not None
    xt = x.reshape(grid[0], grid[1] * tile, D_MODEL)

    tile_spec = pl.BlockSpec((None, tile, D_MODEL), lambda b, t: (b, t, 0))

    def per_batch(shape, **kw):
        return pl.BlockSpec((n_seg,) + shape, lambda b, t: (b, 0, 0), **kw)

    inputs = [xt]
    in_specs = [tile_spec]
    if has_cache:
        inputs += list(caches)
        once = dict(pipeline_mode=pl.Buffered(1))
        in_specs += [per_batch((CONV_W - 1, C_CONV), **once), per_batch((SSM_CONV_W - 1, D_XBC), **once),
                     per_batch((D_STATE, D_SSM), **once)]
    consts = [p["norm_mix"], p["w_in"], p["w_dt"], p["dt_bias"], p["a_log"],
              p["conv_a_w"], p["conv_a_b"], p["ln_a_g"], p["ln_a_b"], p["ssm_conv_w"], p["ssm_conv_b"],
              p["d_x"], p["ssm_norm"], p["w_out"], p["expand"]]
    inputs += consts
    in_specs += [_const_spec(c) for c in consts]

    out_shape = (
        jax.ShapeDtypeStruct(xt.shape, F32),
        jax.ShapeDtypeStruct((n_streams, CONV_W - 1, C_CONV), F32),
        jax.ShapeDtypeStruct((n_streams, SSM_CONV_W - 1, D_XBC), F32),
        jax.ShapeDtypeStruct((n_streams, D_STATE, D_SSM), F32),
    )
    out_specs = (tile_spec, per_batch((CONV_W - 1, C_CONV)), per_batch((SSM_CONV_W - 1, D_XBC)),
                 per_batch((D_STATE, D_SSM)))
    scratch = [
        pltpu.VMEM((tile, D_MODEL), BF16),
        pltpu.VMEM((n_seg, C_CONV // LANES, PAD_A + s, LANES), F32),
        pltpu.VMEM((n_seg, D_XBC // LANES, PAD_B + s, LANES), F32),
        pltpu.VMEM((tile, C_CONV), F32),
        pltpu.VMEM((tile, D_XBC), F32),
        pltpu.VMEM((tile, D_SSM), F32),
        pltpu.VMEM((tile, LANES), F32),
        pltpu.VMEM((tile, D_SSM), F32),
        pltpu.VMEM((tile, C_CONV), BF16),
    ]
    kern = functools.partial(_mixer_kernel, n_seg=n_seg, seg_rows=s, has_cache=has_cache, n_steps=grid[1])
    x1, ta, tb, st = pl.pallas_call(
        kern, grid=grid, in_specs=in_specs, out_specs=out_specs, out_shape=out_shape,
        scratch_shapes=scratch, name="mixer_cache" if has_cache else "mixer_fresh",
        compiler_params=_compiler_params(),
    )(*inputs)
    return x1.reshape(x.shape), ta, tb, st


def _ffn(x1, cache, p, norm_final):
    n_streams, length, _ = x1.shape
    tile, n_seg, s, grid = _layout(n_streams, length, FFN_TILE)
    has_cache = cache is not None
    xt = x1.reshape(grid[0], grid[1] * tile, D_MODEL)
    tile_spec = pl.BlockSpec((None, tile, D_MODEL), lambda b, t: (b, t, 0))
    tail_spec = pl.BlockSpec((n_seg, FFN_CONV_W - 1, D_FF), lambda b, t: (b, 0, 0))

    inputs = [xt]
    in_specs = [tile_spec]
    if has_cache:
        inputs.append(cache)
        in_specs.append(tail_spec)
    consts = [p["norm_ffn"], p["w_up"], p["ffn_conv_w"], p["ffn_conv_b"], p["w_down"]]
    if norm_final is not None:
        consts.append(norm_final)
    inputs += consts
    in_specs += [_const_spec(c) for c in consts]

    out_shape = (jax.ShapeDtypeStruct(xt.shape, F32),
                 jax.ShapeDtypeStruct((n_streams, FFN_CONV_W - 1, D_FF), F32))
    scratch = [
        pltpu.VMEM((tile, D_MODEL), BF16),
        pltpu.VMEM((n_seg, D_FF // LANES, PAD_F + s, LANES), F32),
        pltpu.VMEM((tile, D_FF), F32),
        pltpu.VMEM((tile, D_FF), BF16),
    ]
    kern = functools.partial(_ffn_kernel, n_seg=n_seg, seg_rows=s, has_cache=has_cache, n_steps=grid[1],
                             final_norm=norm_final is not None)
    y, tf = pl.pallas_call(
        kern, grid=grid, in_specs=in_specs, out_specs=(tile_spec, tail_spec), out_shape=out_shape,
        scratch_shapes=scratch, name="ffn_cache" if has_cache else "ffn_fresh",
        compiler_params=_compiler_params(),
    )(*inputs)
    return y.reshape(x1.shape), tf


def _pad_lanes(v, reps=1):
    v = jnp.tile(v.astype(F32), reps)
    return jnp.pad(v, (0, LANES - v.shape[0]))[None, :]


def _pack_kernel(w_ref, o_ref):
    o_ref[...] = pltpu.bitcast(w_ref[...].astype(BF16), jnp.uint32)


def _pack_bf16(w, n_cols=None):
    k, n = w.shape
    n_cols = n if n_cols is None else n_cols
    bn = min(PACK_COLS, n_cols)
    assert k % (2 * SUBLANES) == 0 and n_cols % bn == 0
    return pl.pallas_call(
        _pack_kernel, grid=(n_cols // bn,),
        in_specs=[pl.BlockSpec((k, bn), lambda c: (0, c))],
        out_specs=pl.BlockSpec((k // 2, bn), lambda c: (0, c)),
        out_shape=jax.ShapeDtypeStruct((k // 2, n_cols), jnp.uint32), name="pack_weights",
        compiler_params=pltpu.CompilerParams(dimension_semantics=("arbitrary",)),
    )(w)


def _layer_params(norm_mix, w_in, conv_a_w, conv_a_b, ln_a_g, ln_a_b, ssm_conv_w, ssm_conv_b, dt_bias, a_log,
                  d_skip, ssm_norm, w_out, norm_ffn, w_up, ffn_conv_w, ffn_conv_b, w_down):
    w_dt = w_in[:, O_DT:O_DT + SSM_HEADS]
    w_dt = jnp.pad(jnp.concatenate([w_dt, w_dt], axis=1), ((0, 0), (0, LANES - 2 * SSM_HEADS)))
    head_of_col = jnp.arange(D_SSM, dtype=jnp.int32) // SSM_HEAD_DIM
    expand = (jnp.arange(LANES, dtype=jnp.int32)[:, None] == head_of_col[None, :]).astype(F32)
    return {
        "norm_mix": norm_mix[None, :],
        "w_in": _pack_bf16(w_in, O_DT),
        "w_dt": _pack_bf16(w_dt),
        "dt_bias": _pad_lanes(dt_bias, 2),
        "a_log": _pad_lanes(a_log),
        "conv_a_w": conv_a_w, "conv_a_b": conv_a_b[None, :],
        "ln_a_g": ln_a_g[None, :], "ln_a_b": ln_a_b[None, :],
        "ssm_conv_w": ssm_conv_w, "ssm_conv_b": ssm_conv_b[None, :],
        "d_x": jnp.repeat(d_skip, SSM_HEAD_DIM)[None, :],
        "ssm_norm": ssm_norm[None, :],
        "w_out": _pack_bf16(w_out),
        "expand": _pack_bf16(expand),
        "norm_ffn": norm_ffn[None, :],
        "w_up": _pack_bf16(w_up),
        "ffn_conv_w": ffn_conv_w, "ffn_conv_b": ffn_conv_b[None, :],
        "w_down": _pack_bf16(w_down),
    }


def _state_to_kernel(st):
    n = st.shape[0]
    return jnp.transpose(st, (0, 3, 1, 2)).reshape(n, D_STATE, D_SSM)


def _state_from_kernel(st):
    n = st.shape[0]
    return jnp.transpose(st.reshape(n, D_STATE, SSM_HEADS, SSM_HEAD_DIM), (0, 2, 3, 1))


def kernel(x_prompt, x_sample, cache_conv_a, cache_ssm_conv, state_ssm, cache_ffn_conv, norm_mix, w_in, conv_a_w,
           conv_a_b, ln_a_g, ln_a_b, ssm_conv_w, ssm_conv_b, dt_bias, a_log, d_skip, ssm_norm, w_out, norm_ffn,
           w_up, ffn_conv_w, ffn_conv_b, w_down, norm_final):
    depth = w_in.shape[0]
    xp, xs = x_prompt, x_sample
    outs = [[] for _ in range(8)]
    for i in range(depth):
        p = _layer_params(norm_mix[i], w_in[i], conv_a_w[i], conv_a_b[i], ln_a_g[i], ln_a_b[i], ssm_conv_w[i],
                          ssm_conv_b[i], dt_bias[i], a_log[i], d_skip[i], ssm_norm[i], w_out[i], norm_ffn[i],
                          w_up[i], ffn_conv_w[i], ffn_conv_b[i], w_down[i])
        nfin = norm_final[None, :] if i == depth - 1 else None
        xp, pa, pb, ph = _mixer(xp, None, p)
        xp, pf = _ffn(xp, None, p, nfin)
        xs, sa, sb, sh = _mixer(xs, (cache_conv_a[i], cache_ssm_conv[i], _state_to_kernel(state_ssm[i])), p)
        xs, sf = _ffn(xs, cache_ffn_conv[i], p, nfin)
        for lst, v in zip(outs, (pa, pb, _state_from_kernel(ph), pf, sa, sb, _state_from_kernel(sh), sf)):
            lst.append(v)
    return (xp, xs) + tuple(jnp.stack(o) for o in outs)
```
